```python
import math
import jax, jax.numpy as jnp
from jax import lax
import numpy as np

D_MODEL = 2048
BATCH = 1
SEQ = 8192
DEPTH = 4

CHUNK = 64
Q_BLOCK = 128

N_HEADS = 16
Q_LORA = 512
KV_LORA = 512
QK_NOPE = 128
QK_ROPE = 64
V_HEAD = 128
MLA_WIDTH = N_HEADS * V_HEAD
ROPE_THETA = 10000.0

LRU_WIDTH = D_MODEL
LRU_BLOCKS = 16
LRU_BLOCK_DIM = LRU_WIDTH // LRU_BLOCKS
CONV_K = 4
LRU_C = 8.0

PLE_DIM = 256

EPS = 1e-6

IN_SIZES = (Q_LORA, KV_LORA, QK_ROPE, MLA_WIDTH, LRU_WIDTH, LRU_WIDTH, D_MODEL, D_MODEL)
IN_TOTAL = sum(IN_SIZES)
IN_SPLITS = tuple(int(s) for s in np.cumsum(IN_SIZES)[:-1])

kernel_name = "hybrid_mla_rglru_gated_merge"


def rms_norm(x, g):
    xf = x.astype(jnp.float32)
    y = xf * lax.rsqrt(jnp.mean(xf * xf, axis=-1, keepdims=True) + EPS)
    return (y * g.astype(jnp.float32)).astype(x.dtype)


def rope_tables(positions):
    inv_freq = ROPE_THETA ** (-jnp.arange(0, QK_ROPE, 2, dtype=jnp.float32) / QK_ROPE)
    ang = positions.astype(jnp.float32)[..., None] * inv_freq
    return jnp.cos(ang), jnp.sin(ang)


def apply_rope(x, cos, sin):
    x1, x2 = jnp.split(x, 2, axis=-1)
    c = cos.astype(x.dtype)
    s = sin.astype(x.dtype)
    return jnp.concatenate([x1 * c - x2 * s, x2 * c + x1 * s], axis=-1)


def mla_branch(q_lat, kv_lat, k_rope_raw, gate, cos, sin, q_a_norm, w_q_b, kv_a_norm, w_kv_b, w_o):
    B, S, _ = q_lat.shape
    q = (rms_norm(q_lat, q_a_norm) @ w_q_b).reshape(B, S, N_HEADS, QK_NOPE + QK_ROPE)
    q_nope, q_rope = q[..., :QK_NOPE], q[..., QK_NOPE:]
    q_rope = apply_rope(q_rope, cos[:, :, None, :], sin[:, :, None, :])
    kv = (rms_norm(kv_lat, kv_a_norm) @ w_kv_b).reshape(B, S, N_HEADS, QK_NOPE + V_HEAD)
    k_nope, v = kv[..., :QK_NOPE], kv[..., QK_NOPE:]
    k_rope = apply_rope(k_rope_raw, cos, sin)
    scale = 1.0 / math.sqrt(QK_NOPE + QK_ROPE)
    n_blk = S // Q_BLOCK
    k_chunk = jnp.arange(S) // CHUNK

    qn_b = q_nope.reshape(B, n_blk, Q_BLOCK, N_HEADS, QK_NOPE).transpose(1, 0, 2, 3, 4)
    qr_b = q_rope.reshape(B, n_blk, Q_BLOCK, N_HEADS, QK_ROPE).transpose(1, 0, 2, 3, 4)

    def attend(args):
        qn, qr, blk = args
        s = (jnp.einsum('bqhd,bkhd->bhqk', qn, k_nope)
             + jnp.einsum('bqhr,bkr->bhqk', qr, k_rope)).astype(jnp.float32) * scale
        q_chunk = (blk * Q_BLOCK + jnp.arange(Q_BLOCK)) // CHUNK
        mask = k_chunk[None, :] <= q_chunk[:, None]
        s = jnp.where(mask[None, None], s, -jnp.inf)
        pr = jax.nn.softmax(s, axis=-1).astype(v.dtype)
        return jnp.einsum('bhqk,bkhd->bqhd', pr, v)

    o = lax.map(attend, (qn_b, qr_b, jnp.arange(n_blk)))
    o = o.transpose(1, 0, 2, 3, 4).reshape(B, S, MLA_WIDTH)
    return (o * jax.nn.silu(gate)) @ w_o


def causal_dwconv(x, w, b):
    S = x.shape[1]
    xp = jnp.pad(x, ((0, 0), (CONV_K - 1, 0), (0, 0)))
    y = xp[:, 0:S, :] * w[0]
    for k in range(1, CONV_K):
        y = y + xp[:, k:k + S, :] * w[k]
    return y + b


def rglru_branch(u, gate, conv_w, conv_b, w_rg, b_rg, w_ig, b_ig, lam, w_o):
    B, S, _ = u.shape
    xc = causal_dwconv(u, conv_w, conv_b)
    xh = xc.reshape(B, S, LRU_BLOCKS, LRU_BLOCK_DIM)
    r = jax.nn.sigmoid(jnp.einsum('bshi,hij->bshj', xh, w_rg).reshape(B, S, LRU_WIDTH) + b_rg)
    i = jax.nn.sigmoid(jnp.einsum('bshi,hij->bshj', xh, w_ig).reshape(B, S, LRU_WIDTH) + b_ig)
    log_a = (-LRU_C * r.astype(jnp.float32)) * jax.nn.softplus(-lam.astype(jnp.float32))
    a = jnp.exp(log_a)
    mult = jnp.sqrt(-jnp.expm1(2.0 * log_a))
    bterm = mult * (i * xc).astype(jnp.float32)

    def combine(lhs, rhs):
        a1, b1 = lhs
        a2, b2 = rhs
        return a1 * a2, a2 * b1 + b2

    _, h = lax.associative_scan(combine, (a, bterm), axis=1)
    h = h.astype(u.dtype)
    return (h * jax.nn.silu(gate)) @ w_o


def setup_inputs(seed: int = 0) -> dict:
    key = jax.random.key(seed)
    ks = jax.random.split(key, 32)
    f32 = jnp.float32

    def nrm(k, shape, fan_in):
        return jax.random.normal(k, shape, f32) * (fan_in ** -0.5)

    def gain(k, shape):
        return 1.0 + 0.02 * jax.random.normal(k, shape, f32)

    x = jax.random.normal(ks[0], (BATCH, SEQ, D_MODEL), f32)
    p = jax.random.normal(ks[1], (DEPTH, BATCH, SEQ, PLE_DIM), f32)
    offset = jax.random.randint(ks[2], (BATCH, 1), 0, 4096, dtype=jnp.int32)
    positions = offset + jnp.arange(SEQ, dtype=jnp.int32)[None, :]

    u = jax.random.uniform(ks[3], (DEPTH, LRU_WIDTH), f32, 0.9, 0.999)
    a0 = u ** (1.0 / LRU_C)
    lru_lambda = jnp.log(a0) - jnp.log1p(-a0)

    return {
        "x": x,
        "p": p,
        "positions": positions,
        "attn_norm": gain(ks[4], (DEPTH, D_MODEL)),
        "w_in": nrm(ks[5], (DEPTH, D_MODEL, IN_TOTAL), D_MODEL),
        "q_a_norm": gain(ks[6], (DEPTH, Q_LORA)),
        "w_q_b": nrm(ks[7], (DEPTH, Q_LORA, N_HEADS * (QK_NOPE + QK_ROPE)), Q_LORA),
        "kv_a_norm": gain(ks[8], (DEPTH, KV_LORA)),
        "w_kv_b": nrm(ks[9], (DEPTH, KV_LORA, N_HEADS * (QK_NOPE + V_HEAD)), KV_LORA),
        "conv_w": nrm(ks[10], (DEPTH, CONV_K, LRU_WIDTH), CONV_K),
        "conv_b": 0.01 * jax.random.normal(ks[11], (DEPTH, LRU_WIDTH), f32),
        "w_rg": nrm(ks[12], (DEPTH, LRU_BLOCKS, LRU_BLOCK_DIM, LRU_BLOCK_DIM), LRU_BLOCK_DIM),
        "b_rg": 0.01 * jax.random.normal(ks[13], (DEPTH, LRU_WIDTH), f32),
        "w_ig": nrm(ks[14], (DEPTH, LRU_BLOCKS, LRU_BLOCK_DIM, LRU_BLOCK_DIM), LRU_BLOCK_DIM),
        "b_ig": 0.01 * jax.random.normal(ks[15], (DEPTH, LRU_WIDTH), f32),
        "lru_lambda": lru_lambda,
        "w_o_mla": nrm(ks[16], (DEPTH, MLA_WIDTH, D_MODEL), MLA_WIDTH),
        "w_o_lru": nrm(ks[17], (DEPTH, LRU_WIDTH, D_MODEL), LRU_WIDTH),
        "w_out": nrm(ks[18], (DEPTH, D_MODEL, D_MODEL), D_MODEL),
        "ple_norm": gain(ks[19], (DEPTH, D_MODEL)),
        "w_ple_gate": nrm(ks[20], (DEPTH, D_MODEL, D_MODEL), D_MODEL),
        "w_ple": nrm(ks[21], (DEPTH, PLE_DIM, D_MODEL), PLE_DIM),
        "final_norm": gain(ks[22], (D_MODEL,)),
    }


def reference(x, p, positions, attn_norm, w_in, q_a_norm, w_q_b, kv_a_norm, w_kv_b,
              conv_w, conv_b, w_rg, b_rg, w_ig, b_ig, lru_lambda,
              w_o_mla, w_o_lru, w_out, ple_norm, w_ple_gate, w_ple, final_norm):
    cos, sin = rope_tables(positions)
    for l in range(DEPTH):
        h = rms_norm(x, attn_norm[l])
        z = h @ w_in[l]
        q_lat, kv_lat, k_rope_raw, g_mla, u_lru, g_lru, m_mla, m_lru = jnp.split(z, IN_SPLITS, axis=-1)
        y_mla = mla_branch(q_lat, kv_lat, k_rope_raw, g_mla, cos, sin,
                           q_a_norm[l], w_q_b[l], kv_a_norm[l], w_kv_b[l], w_o_mla[l])
        y_lru = rglru_branch(u_lru, g_lru, conv_w[l], conv_b[l], w_rg[l], b_rg[l],
                             w_ig[l], b_ig[l], lru_lambda[l], w_o_lru[l])
        merged = jax.nn.sigmoid(m_mla) * y_mla + jax.nn.sigmoid(m_lru) * y_lru
        x = x + merged @ w_out[l]
        ple_gate = jax.nn.sigmoid(rms_norm(x, ple_norm[l]) @ w_ple_gate[l])
        x = x + (p[l] @ w_ple[l]) * ple_gate
    return rms_norm(x, final_norm)
```

```python
import functools
import math

import jax
import jax.numpy as jnp
from jax import lax
from jax.experimental import pallas as pl
from jax.experimental.pallas import tpu as pltpu

F32 = jnp.float32
BF16 = jnp.bfloat16

D_MODEL = 2048
SEQ = 8192
DEPTH = 4
CHUNK = 64
N_HEADS = 16
Q_LORA = 512
KV_LORA = 512
QK_NOPE = 128
QK_ROPE = 64
V_HEAD = 128
ROPE_THETA = 10000.0
LRU_BLOCKS = 16
LRU_BLOCK_DIM = 128
CONV_K = 4
LRU_C = 8.0
PLE_DIM = 256
EPS = 1e-6

O_QLAT, O_KVLAT, O_KR, O_GMLA, O_U, O_GLRU, O_MMLA, O_MLRU, O_END = (
    0, 512, 1024, 1088, 3136, 5184, 7232, 9280, 11328)

LANES = 128
SUBLANES = 8
HEAD_PAD = 256
VMEM_LIMIT = 56 * 1024 * 1024

TM = 256
TQ = 512
TK = 512
NCH = 512


def _resident(shape):
    return pl.BlockSpec(shape, lambda *_: (0,) * len(shape), pipeline_mode=pl.Buffered(1))


def _params(*sem):
    return pltpu.CompilerParams(dimension_semantics=sem, vmem_limit_bytes=VMEM_LIMIT)


def _sigmoid(x):
    return 1.0 / (1.0 + jnp.exp(-x))


def _silu(x):
    return x * _sigmoid(x)


def _rms(x, g):
    return x * lax.rsqrt(jnp.mean(x * x, axis=-1, keepdims=True) + EPS) * g


def _dot(a, b):
    return jnp.dot(a, b, preferred_element_type=F32)


def _rope_kernel(pos_ref, invf_ref, cs_ref):
    ang = pos_ref[...].astype(F32) * invf_ref[...]
    lane = lax.broadcasted_iota(jnp.int32, ang.shape, 1)
    c = jnp.cos(ang)
    s = jnp.sin(ang)
    cs_ref[...] = jnp.where(lane < 64, c, jnp.where(lane < 96, -s, s))


def _rope_table(positions):
    rows = 1024
    inv_freq = ROPE_THETA ** (-jnp.arange(0, QK_ROPE, 2, dtype=F32) / QK_ROPE)
    invf = jnp.tile(inv_freq, 4).reshape(1, LANES)
    pos = positions.reshape(SEQ, 1)
    return pl.pallas_call(
        _rope_kernel,
        grid=(SEQ // rows,),
        in_specs=[pl.BlockSpec((rows, 1), lambda i: (i, 0)), _resident((1, LANES))],
        out_specs=pl.BlockSpec((rows, LANES), lambda i: (i, 0)),
        out_shape=jax.ShapeDtypeStruct((SEQ, LANES), F32),
        compiler_params=_params("parallel"),
        name="rope_table",
    )(pos, invf)


def _norm_kernel(x_ref, g_ref, h_ref):
    h_ref[...] = _rms(x_ref[...], g_ref[...]).astype(h_ref.dtype)


def _first_norm(x, g):
    return pl.pallas_call(
        _norm_kernel,
        grid=(SEQ // TM,),
        in_specs=[pl.BlockSpec((TM, D_MODEL), lambda i: (i, 0)), _resident((1, D_MODEL))],
        out_specs=pl.BlockSpec((TM, D_MODEL), lambda i: (i, 0)),
        out_shape=jax.ShapeDtypeStruct((SEQ, D_MODEL), BF16),
        compiler_params=_params("parallel"),
        name="first_norm",
    )(x, g)


def _qkv_kernel(h_ref, wlat_ref, qg_ref, kvg_ref, wqn_ref, wqr_ref, wkn_ref, wv_ref, cs_ref,
                q_ref, k_ref, v_ref):
    scale = 1.0 / math.sqrt(QK_NOPE + QK_ROPE)
    lat = _dot(h_ref[...], wlat_ref[...])
    qn = _rms(lat[:, 0:Q_LORA], qg_ref[...]).astype(BF16)
    kvn = _rms(lat[:, Q_LORA:Q_LORA + KV_LORA], kvg_ref[...]).astype(BF16)
    cs = cs_ref[...]
    lane = lax.broadcasted_iota(jnp.int32, cs.shape, 1)
    kr = lat[:, 2 * Q_LORA:2 * Q_LORA + LANES] * cs
    kr = kr + pltpu.roll(kr, 64, 1)
    kr = jnp.where(lane < 64, kr, 0.0).astype(BF16)
    for hp in range(N_HEADS // 2):
        cols = slice(hp * 256, (hp + 1) * 256)
        qnope = _dot(qn, wqn_ref[:, cols]) * scale
        qrope = _dot(qn, wqr_ref[:, cols])
        knope = _dot(kvn, wkn_ref[:, cols])
        val = _dot(kvn, wv_ref[:, cols])
        for j in range(2):
            hd = 2 * hp + j
            sub = slice(j * LANES, (j + 1) * LANES)
            r = qrope[:, sub] * cs
            r = (r + pltpu.roll(r, 64, 1)) * scale
            q_ref[hd, :, 0:LANES] = qnope[:, sub].astype(BF16)
            q_ref[hd, :, LANES:HEAD_PAD] = r.astype(BF16)
            k_ref[hd, :, 0:LANES] = knope[:, sub].astype(BF16)
            k_ref[hd, :, LANES:HEAD_PAD] = kr
            v_ref[hd] = val[:, sub].astype(BF16)


def _qkv(h, wl, cs):
    return pl.pallas_call(
        _qkv_kernel,
        grid=(SEQ // TM,),
        in_specs=[
            pl.BlockSpec((TM, D_MODEL), lambda i: (i, 0)),
            _resident((D_MODEL, 2 * Q_LORA + LANES)),
            _resident((1, Q_LORA)), _resident((1, KV_LORA)),
            _resident((Q_LORA, D_MODEL)), _resident((Q_LORA, D_MODEL)),
            _resident((KV_LORA, D_MODEL)), _resident((KV_LORA, D_MODEL)),
            pl.BlockSpec((TM, LANES), lambda i: (i, 0)),
        ],
        out_specs=[
            pl.BlockSpec((N_HEADS, TM, HEAD_PAD), lambda i: (0, i, 0)),
            pl.BlockSpec((N_HEADS, TM, HEAD_PAD), lambda i: (0, i, 0)),
            pl.BlockSpec((N_HEADS, TM, V_HEAD), lambda i: (0, i, 0)),
        ],
        out_shape=[
            jax.ShapeDtypeStruct((N_HEADS, SEQ, HEAD_PAD), BF16),
            jax.ShapeDtypeStruct((N_HEADS, SEQ, HEAD_PAD), BF16),
            jax.ShapeDtypeStruct((N_HEADS, SEQ, V_HEAD), BF16),
        ],
        compiler_params=_params("parallel"),
        name="qkv",
    )(h, wl["w_lat"], wl["q_a_norm"], wl["kv_a_norm"], wl["w_qn"], wl["w_qr"], wl["w_kn"],
      wl["w_v"], cs)


def _attn_kernel(q_ref, k_ref, v_ref, o_ref):
    qi = pl.program_id(1)
    q = q_ref[0]

    def step(kb, carry, masked):
        m, l, acc = carry
        rows = pl.ds(pl.multiple_of(kb * TK, TK), TK)
        s = lax.dot_general(q, k_ref[0, rows, :], (((1,), (1,)), ((), ())),
                            preferred_element_type=F32)
        if masked:
            r = lax.broadcasted_iota(jnp.int32, s.shape, 0) // CHUNK
            c = lax.broadcasted_iota(jnp.int32, s.shape, 1) // CHUNK
            s = jnp.where(c <= r, s, -jnp.inf)
        m_new = jnp.maximum(m, jnp.max(s, axis=1, keepdims=True))
        alpha = jnp.exp(m - m_new)
        p = jnp.exp(s - m_new)
        l = alpha * l + jnp.sum(p, axis=1, keepdims=True)
        acc = alpha * acc + _dot(p.astype(BF16), v_ref[0, rows, :])
        return m_new, l, acc

    init = (jnp.full((TQ, 1), -jnp.inf, F32), jnp.zeros((TQ, 1), F32),
            jnp.zeros((TQ, V_HEAD), F32))
    carry = lax.fori_loop(0, qi, lambda kb, c: step(kb, c, False), init)
    m, l, acc = step(qi, carry, True)
    o_ref[...] = (acc / l).astype(o_ref.dtype)


def _attention(q, k, v):
    return pl.pallas_call(
        _attn_kernel,
        grid=(N_HEADS, SEQ // TQ),
        in_specs=[
            pl.BlockSpec((1, TQ, HEAD_PAD), lambda h, i: (h, i, 0)),
            pl.BlockSpec((1, SEQ, HEAD_PAD), lambda h, i: (h, 0, 0)),
            pl.BlockSpec((1, SEQ, V_HEAD), lambda h, i: (h, 0, 0)),
        ],
        out_specs=pl.BlockSpec((TQ, V_HEAD), lambda h, i: (i, h)),
        out_shape=jax.ShapeDtypeStruct((SEQ, N_HEADS * V_HEAD), F32),
        compiler_params=_params("parallel", "parallel"),
        name="attn",
    )(q, k, v)


def _mla_out_kernel(h_ref, o_ref, wgm_ref, wo_ref, a_ref, og_ref):
    h = h_ref[...]
    for c in range(D_MODEL // NCH):
        cols = slice(c * NCH, (c + 1) * NCH)
        g = _dot(h, wgm_ref[:, cols])
        og_ref[:, cols] = (o_ref[:, cols] * _silu(g)).astype(BF16)
    og = og_ref[...]
    for c in range(D_MODEL // NCH):
        cols = slice(c * NCH, (c + 1) * NCH)
        mm = _dot(h, wgm_ref[:, D_MODEL + c * NCH:D_MODEL + (c + 1) * NCH])
        a_ref[:, cols] = _sigmoid(mm) * _dot(og, wo_ref[:, cols])


def _mla_out(h, o, wl):
    return pl.pallas_call(
        _mla_out_kernel,
        grid=(SEQ // TM,),
        in_specs=[
            pl.BlockSpec((TM, D_MODEL), lambda i: (i, 0)),
            pl.BlockSpec((TM, D_MODEL), lambda i: (i, 0)),
            _resident((D_MODEL, 2 * D_MODEL)),
            _resident((D_MODEL, D_MODEL)),
        ],
        out_specs=pl.BlockSpec((TM, D_MODEL), lambda i: (i, 0)),
        out_shape=jax.ShapeDtypeStruct((SEQ, D_MODEL), F32),
        scratch_shapes=[pltpu.VMEM((TM, D_MODEL), BF16)],
        compiler_params=_params("parallel"),
        name="mla_out",
    )(h, o, wl["w_gm_mm"], wl["w_o_mla"])


def _lru_kernel(h_ref, win_ref, cw_ref, cb_ref, wg_ref, brg_ref, big_ref, lam_ref, wo_ref,
                b_out_ref, ubuf, a_s, b_s, hg_s, hcar):
    i = pl.program_id(0)

    @pl.when(i == 0)
    def _():
        ubuf[0:SUBLANES, :] = jnp.zeros((SUBLANES, D_MODEL), F32)
        hcar[...] = jnp.zeros_like(hcar)

    h = h_ref[...]
    for c in range(D_MODEL // NCH):
        cols = slice(c * NCH, (c + 1) * NCH)
        ubuf[SUBLANES:SUBLANES + TM, cols] = _dot(h, win_ref[:, cols])

    nl = -lam_ref[...]
    softplus = jnp.maximum(nl, 0.0) + jnp.log1p(jnp.exp(-jnp.abs(nl)))

    for j in range(LRU_BLOCKS):
        cols = slice(j * LRU_BLOCK_DIM, (j + 1) * LRU_BLOCK_DIM)
        xc = cb_ref[:, cols] + cw_ref[CONV_K - 1:CONV_K, cols] * ubuf[SUBLANES:SUBLANES + TM, cols]
        for kk in range(CONV_K - 1):
            off = SUBLANES - (CONV_K - 1) + kk
            xc = xc + cw_ref[kk:kk + 1, cols] * ubuf[off:off + TM, cols]
        ri = _dot(xc.astype(BF16), wg_ref[j])
        r = _sigmoid(ri[:, 0:LRU_BLOCK_DIM] + brg_ref[:, cols])
        ig = _sigmoid(ri[:, LRU_BLOCK_DIM:] + big_ref[:, cols])
        log_a = (-LRU_C * r) * softplus[:, cols]
        a = jnp.exp(log_a)
        a_s[:, cols] = a
        b_s[:, cols] = jnp.sqrt(-jnp.tanh(log_a) * (a * a + 1.0)) * (ig * xc)

    ubuf[0:SUBLANES, :] = ubuf[TM:TM + SUBLANES, :]

    SW = 512
    row = lax.broadcasted_iota(jnp.int32, (SUBLANES, SW), 0)
    for c in range(D_MODEL // SW):
        cols = slice(c * SW, (c + 1) * SW)

        def scan_body(rg, hprev, cols=cols):
            rows = pl.ds(pl.multiple_of(rg * SUBLANES, SUBLANES), SUBLANES)
            a8 = a_s[rows, cols]
            b8 = b_s[rows, cols]
            for d in (1, 2, 4):
                keep = row >= d
                b8 = jnp.where(keep, a8 * pltpu.roll(b8, d, 0) + b8, b8)
                a8 = jnp.where(keep, a8 * pltpu.roll(a8, d, 0), a8)
            hs = a8 * hprev + b8
            b_s[rows, cols] = hs
            return jnp.broadcast_to(hs[SUBLANES - 1:SUBLANES, :], (SUBLANES, SW))

        hcar[:, cols] = lax.fori_loop(0, TM // SUBLANES, scan_body, hcar[:, cols])

    for c in range(D_MODEL // NCH):
        cols = slice(c * NCH, (c + 1) * NCH)
        g = _dot(h, win_ref[:, D_MODEL + c * NCH:D_MODEL + (c + 1) * NCH])
        hg_s[:, cols] = (b_s[:, cols] * _silu(g)).astype(BF16)
    hg = hg_s[...]
    for c in range(D_MODEL // NCH):
        cols = slice(c * NCH, (c + 1) * NCH)
        ml = _dot(h, win_ref[:, 2 * D_MODEL + c * NCH:2 * D_MODEL + (c + 1) * NCH])
        b_out_ref[:, cols] = _sigmoid(ml) * _dot(hg, wo_ref[:, cols])


def _lru(h, wl):
    return pl.pallas_call(
        _lru_kernel,
        grid=(SEQ // TM,),
        in_specs=[
            pl.BlockSpec((TM, D_MODEL), lambda i: (i, 0)),
            _resident((D_MODEL, 3 * D_MODEL)),
            _resident((CONV_K, D_MODEL)), _resident((1, D_MODEL)),
            _resident((LRU_BLOCKS, LRU_BLOCK_DIM, 2 * LRU_BLOCK_DIM)),
            _resident((1, D_MODEL)), _resident((1, D_MODEL)), _resident((1, D_MODEL)),
            _resident((D_MODEL, D_MODEL)),
        ],
        out_specs=pl.BlockSpec((TM, D_MODEL), lambda i: (i, 0)),
        out_shape=jax.ShapeDtypeStruct((SEQ, D_MODEL), F32),
        scratch_shapes=[
            pltpu.VMEM((TM + SUBLANES, D_MODEL), F32),
            pltpu.VMEM((TM, D_MODEL), F32),
            pltpu.VMEM((TM, D_MODEL), F32),
            pltpu.VMEM((TM, D_MODEL), BF16),
            pltpu.VMEM((SUBLANES, D_MODEL), F32),
        ],
        compiler_params=_params("arbitrary"),
        name="lru",
    )(h, wl["w_lru_in"], wl["conv_w"], wl["conv_b"], wl["w_gates"], wl["b_rg"], wl["b_ig"],
      wl["lru_lambda"], wl["w_o_lru"])


def _final_kernel(x_ref, a_ref, b_ref, p_ref, wout_ref, pg_ref, wpg_ref, wple_ref, ng_ref,
                  *out_refs, last):
    merged = (a_ref[...] + b_ref[...]).astype(BF16)
    x1 = x_ref[...] + _dot(merged, wout_ref[...])
    n1 = _rms(x1, pg_ref[...]).astype(BF16)
    gate = _sigmoid(_dot(n1, wpg_ref[...]))
    x2 = x1 + _dot(p_ref[...].astype(BF16), wple_ref[...]) * gate
    normed = _rms(x2, ng_ref[...])
    if last:
        out_refs[0][...] = normed
    else:
        out_refs[0][...] = x2
        out_refs[1][...] = normed.astype(BF16)


def _final(x, a, b, p, layer, wl, next_gain, last):
    row = pl.BlockSpec((TM, D_MODEL), lambda i: (i, 0))
    if last:
        out_specs = [row]
        out_shape = [jax.ShapeDtypeStruct((SEQ, D_MODEL), F32)]
    else:
        out_specs = [row, row]
        out_shape = [jax.ShapeDtypeStruct((SEQ, D_MODEL), F32),
                     jax.ShapeDtypeStruct((SEQ, D_MODEL), BF16)]
    return pl.pallas_call(
        functools.partial(_final_kernel, last=last),
        grid=(SEQ // TM,),
        in_specs=[
            row, row, row,
            pl.BlockSpec((None, None, TM, PLE_DIM), lambda i: (layer, 0, i, 0)),
            _resident((D_MODEL, D_MODEL)), _resident((1, D_MODEL)),
            _resident((D_MODEL, D_MODEL)), _resident((PLE_DIM, D_MODEL)),
            _resident((1, D_MODEL)),
        ],
        out_specs=out_specs,
        out_shape=out_shape,
        compiler_params=_params("parallel"),
        name="final",
    )(x, a, b, p, wl["w_out"], wl["ple_norm"], wl["w_ple_gate"], wl["w_ple"], next_gain)


def _swap_halves(w):
    half = QK_ROPE // 2
    return jnp.concatenate([w[..., half:], w[..., :half]], axis=-1)


def _layer_weights(l, w_in, q_a_norm, w_q_b, kv_a_norm, w_kv_b, conv_w, conv_b, w_rg, b_rg,
                   w_ig, b_ig, lru_lambda, w_o_mla, w_o_lru, w_out, ple_norm, w_ple_gate, w_ple):
    wi = w_in[l]
    w_kr = wi[:, O_KR:O_GMLA]
    wq = w_q_b[l].reshape(Q_LORA, N_HEADS, QK_NOPE + QK_ROPE)
    wq_r = wq[:, :, QK_NOPE:]
    wkv = w_kv_b[l].reshape(KV_LORA, N_HEADS, QK_NOPE + V_HEAD)
    row = lambda v: v[l].reshape(1, -1)
    return {
        "w_lat": jnp.concatenate([wi[:, :O_KR], w_kr, _swap_halves(w_kr)], axis=1).astype(BF16),
        "q_a_norm": row(q_a_norm), "kv_a_norm": row(kv_a_norm),
        "w_qn": wq[:, :, :QK_NOPE].reshape(Q_LORA, D_MODEL).astype(BF16),
        "w_qr": jnp.concatenate([wq_r, _swap_halves(wq_r)], axis=-1).reshape(Q_LORA, D_MODEL).astype(BF16),
        "w_kn": wkv[:, :, :QK_NOPE].reshape(KV_LORA, D_MODEL).astype(BF16),
        "w_v": wkv[:, :, QK_NOPE:].reshape(KV_LORA, D_MODEL).astype(BF16),
        "w_gm_mm": jnp.concatenate([wi[:, O_GMLA:O_U], wi[:, O_MMLA:O_MLRU]], axis=1).astype(BF16),
        "w_o_mla": w_o_mla[l].astype(BF16),
        "w_lru_in": jnp.concatenate([wi[:, O_U:O_MMLA], wi[:, O_MLRU:O_END]], axis=1).astype(BF16),
        "conv_w": conv_w[l], "conv_b": row(conv_b),
        "w_gates": jnp.concatenate([w_rg[l], w_ig[l]], axis=-1).astype(BF16),
        "b_rg": row(b_rg), "b_ig": row(b_ig), "lru_lambda": row(lru_lambda),
        "w_o_lru": w_o_lru[l].astype(BF16),
        "w_out": w_out[l].astype(BF16), "ple_norm": row(ple_norm),
        "w_ple_gate": w_ple_gate[l].astype(BF16), "w_ple": w_ple[l].astype(BF16),
    }


def kernel(x, p, positions, attn_norm, w_in, q_a_norm, w_q_b, kv_a_norm, w_kv_b, conv_w, conv_b,
           w_rg, b_rg, w_ig, b_ig, lru_lambda, w_o_mla, w_o_lru, w_out, ple_norm, w_ple_gate,
           w_ple, final_norm):
    assert x.shape == (1, SEQ, D_MODEL) and p.shape == (DEPTH, 1, SEQ, PLE_DIM)
    cs = _rope_table(positions)
    xs = x.reshape(SEQ, D_MODEL)
    h = _first_norm(xs, attn_norm[0].reshape(1, D_MODEL))
    out = None
    for l in range(DEPTH):
        wl = _layer_weights(l, w_in, q_a_norm, w_q_b, kv_a_norm, w_kv_b, conv_w, conv_b, w_rg,
                            b_rg, w_ig, b_ig, lru_lambda, w_o_mla, w_o_lru, w_out, ple_norm,
                            w_ple_gate, w_ple)
        q, k, v = _qkv(h, wl, cs)
        o = _attention(q, k, v)
        a = _mla_out(h, o, wl)
        b = _lru(h, wl)
        last = l == DEPTH - 1
        gain = (final_norm if last else attn_norm[l + 1]).reshape(1, D_MODEL)
        res = _final(xs, a, b, p, l, wl, gain, last)
        if last:
            out = res[0]
        else:
            xs, h = res
    return out.reshape(1, SEQ, D_MODEL)
```

```python
import functools
import math

import jax
import jax.numpy as jnp
from jax import lax
from jax.experimental import pallas as pl
from jax.experimental.pallas import tpu as pltpu

F32 = jnp.float32
BF16 = jnp.bfloat16

D_MODEL = 2048
SEQ = 8192
DEPTH = 4
CHUNK = 64
N_HEADS = 16
Q_LORA = 512
KV_LORA = 512
QK_NOPE = 128
QK_ROPE = 64
V_HEAD = 128
ROPE_THETA = 10000.0
LRU_BLOCKS = 16
LRU_BLOCK_DIM = 128
CONV_K = 4
LRU_C = 8.0
PLE_DIM = 256
EPS = 1e-6

O_QLAT, O_KVLAT, O_KR, O_GMLA, O_U, O_GLRU, O_MMLA, O_MLRU, O_END = (
    0, 512, 1024, 1088, 3136, 5184, 7232, 9280, 11328)

LANES = 128
SUBLANES = 8
HEAD_PAD = 256
VMEM_LIMIT = 56 * 1024 * 1024

TM = 256
TQ = 1024
TK = 512
NCH = 512


def _resident(shape):
    return pl.BlockSpec(shape, lambda *_: (0,) * len(shape), pipeline_mode=pl.Buffered(1))


def _params(*sem):
    return pltpu.CompilerParams(dimension_semantics=sem, vmem_limit_bytes=VMEM_LIMIT)


def _sigmoid(x):
    return 1.0 / (1.0 + jnp.exp(-x))


def _silu(x):
    return x * _sigmoid(x)


def _rms(x, g):
    return x * lax.rsqrt(jnp.mean(x * x, axis=-1, keepdims=True) + EPS) * g


def _dot(a, b):
    return jnp.dot(a, b, preferred_element_type=F32)


def _rope_kernel(pos_ref, invf_ref, cs_ref):
    ang = pos_ref[...].astype(F32) * invf_ref[...]
    lane = lax.broadcasted_iota(jnp.int32, ang.shape, 1)
    c = jnp.cos(ang)
    s = jnp.sin(ang)
    cs_ref[...] = jnp.where(lane < 64, c, jnp.where(lane < 96, -s, s))


def _rope_table(positions):
    rows = 1024
    inv_freq = ROPE_THETA ** (-jnp.arange(0, QK_ROPE, 2, dtype=F32) / QK_ROPE)
    invf = jnp.tile(inv_freq, 4).reshape(1, LANES)
    pos = positions.reshape(SEQ, 1)
    return pl.pallas_call(
        _rope_kernel,
        grid=(SEQ // rows,),
        in_specs=[pl.BlockSpec((rows, 1), lambda i: (i, 0)), _resident((1, LANES))],
        out_specs=pl.BlockSpec((rows, LANES), lambda i: (i, 0)),
        out_shape=jax.ShapeDtypeStruct((SEQ, LANES), F32),
        compiler_params=_params("parallel"),
        name="rope_table",
    )(pos, invf)


def _norm_kernel(x_ref, g_ref, h_ref):
    h_ref[...] = _rms(x_ref[...], g_ref[...]).astype(h_ref.dtype)


def _first_norm(x, g):
    return pl.pallas_call(
        _norm_kernel,
        grid=(SEQ // TM,),
        in_specs=[pl.BlockSpec((TM, D_MODEL), lambda i: (i, 0)), _resident((1, D_MODEL))],
        out_specs=pl.BlockSpec((TM, D_MODEL), lambda i: (i, 0)),
        out_shape=jax.ShapeDtypeStruct((SEQ, D_MODEL), BF16),
        compiler_params=_params("parallel"),
        name="first_norm",
    )(x, g)


def _qkv_kernel(h_ref, wlat_ref, qg_ref, kvg_ref, wqn_ref, wqr_ref, wkn_ref, wvt_ref, cs_ref,
                q_ref, k_ref, vt_ref):
    scale = math.log2(math.e) / math.sqrt(QK_NOPE + QK_ROPE)
    lat = _dot(h_ref[...], wlat_ref[...])
    qn = _rms(lat[:, 0:Q_LORA], qg_ref[...]).astype(BF16)
    kvn = _rms(lat[:, Q_LORA:Q_LORA + KV_LORA], kvg_ref[...]).astype(BF16)
    cs = cs_ref[...]
    lane = lax.broadcasted_iota(jnp.int32, cs.shape, 1)
    kr = lat[:, 2 * Q_LORA:2 * Q_LORA + LANES] * cs
    kr = kr + pltpu.roll(kr, 64, 1)
    kr = jnp.where(lane < 64, kr, 0.0).astype(BF16)
    vt_ref[0] = lax.dot_general(wvt_ref[...], kvn, (((1,), (1,)), ((), ())),
                                preferred_element_type=F32).astype(BF16)
    for hp in range(N_HEADS // 2):
        cols = slice(hp * 256, (hp + 1) * 256)
        qnope = _dot(qn, wqn_ref[:, cols]) * scale
        qrope = _dot(qn, wqr_ref[:, cols])
        knope = _dot(kvn, wkn_ref[:, cols])
        for j in range(2):
            hd = 2 * hp + j
            sub = slice(j * LANES, (j + 1) * LANES)
            r = qrope[:, sub] * cs
            r = (r + pltpu.roll(r, 64, 1)) * scale
            q_ref[hd, :, 0:LANES] = qnope[:, sub].astype(BF16)
            q_ref[hd, :, LANES:HEAD_PAD] = r.astype(BF16)
            k_ref[hd, :, 0:LANES] = knope[:, sub].astype(BF16)
            k_ref[hd, :, LANES:HEAD_PAD] = kr


def _qkv(h, wl, cs):
    return pl.pallas_call(
        _qkv_kernel,
        grid=(SEQ // TK,),
        in_specs=[
            pl.BlockSpec((TK, D_MODEL), lambda i: (i, 0)),
            _resident((D_MODEL, 2 * Q_LORA + LANES)),
            _resident((1, Q_LORA)), _resident((1, KV_LORA)),
            _resident((Q_LORA, D_MODEL)), _resident((Q_LORA, D_MODEL)),
            _resident((KV_LORA, D_MODEL)), _resident((N_HEADS * V_HEAD, KV_LORA)),
            pl.BlockSpec((TK, LANES), lambda i: (i, 0)),
        ],
        out_specs=[
            pl.BlockSpec((N_HEADS, TK, HEAD_PAD), lambda i: (0, i, 0)),
            pl.BlockSpec((N_HEADS, TK, HEAD_PAD), lambda i: (0, i, 0)),
            pl.BlockSpec((1, N_HEADS * V_HEAD, TK), lambda i: (i, 0, 0)),
        ],
        out_shape=[
            jax.ShapeDtypeStruct((N_HEADS, SEQ, HEAD_PAD), BF16),
            jax.ShapeDtypeStruct((N_HEADS, SEQ, HEAD_PAD), BF16),
            jax.ShapeDtypeStruct((SEQ // TK, N_HEADS * V_HEAD, TK), BF16),
        ],
        compiler_params=_params("parallel"),
        name="qkv",
    )(h, wl["w_lat"], wl["q_a_norm"], wl["kv_a_norm"], wl["w_qn"], wl["w_qr"], wl["w_kn"],
      wl["w_vt"], cs)


def _attn_kernel(q_ref, k_ref, vt_ref, o_ref, s_buf, p_buf, alpha_s, mb_s, m_s, l_s, acc_s):
    qi = pl.program_id(1)
    n_full = qi * (TQ // TK)
    n_blocks = n_full + TQ // TK
    m_s[...] = jnp.full(m_s.shape, -jnp.inf, F32)
    l_s[...] = jnp.zeros(l_s.shape, F32)
    acc_s[...] = jnp.zeros(acc_s.shape, F32)

    def scores(t, par, masked):
        kt = k_ref[0, pl.ds(pl.multiple_of(t * TK, TK), TK), :]
        st = lax.dot_general(kt, q_ref[0], (((1,), (1,)), ((), ())),
                             preferred_element_type=F32)
        if masked:
            key = (t * TK + lax.broadcasted_iota(jnp.int32, st.shape, 0)) // CHUNK
            qry = (qi * TQ + lax.broadcasted_iota(jnp.int32, st.shape, 1)) // CHUNK
            st = jnp.where(key <= qry, st, -jnp.inf)
        s_buf[par] = st
        mb_s[par] = jnp.max(st, axis=0, keepdims=True)

    def softmax(par):
        m_old = m_s[...]
        m_new = jnp.maximum(m_old, mb_s[par])
        alpha = jnp.exp2(m_old - m_new)
        alpha_s[par] = alpha
        m_s[...] = m_new
        m_rows = jnp.broadcast_to(m_new, (SUBLANES, TQ))
        lsum = jnp.zeros((SUBLANES, TQ), F32)
        for r in range(0, TK, 2 * SUBLANES):
            lo = jnp.exp2(s_buf[par, r:r + SUBLANES, :] - m_rows)
            hi = jnp.exp2(s_buf[par, r + SUBLANES:r + 2 * SUBLANES, :] - m_rows)
            lsum = lsum + (lo + hi)
            p_buf[par, r:r + 2 * SUBLANES, :] = jnp.concatenate([lo, hi], axis=0).astype(BF16)
        l_s[...] = alpha * l_s[...] + jnp.sum(lsum, axis=0, keepdims=True)

    def values(t, par):
        for c in range(0, TQ, NCH):
            acc_s[:, c:c + NCH] = (alpha_s[par, :, c:c + NCH] * acc_s[:, c:c + NCH]
                                   + _dot(vt_ref[t], p_buf[par, :, c:c + NCH]))

    def tick(t, par, masked):
        values(t - 2, par)
        softmax(1 - par)
        scores(t, par, masked)

    scores(0, 0, True)
    softmax(0)
    scores(1, 1, True)

    @pl.when(n_full > 0)
    def _():
        def pair(i, carry):
            t = 2 + 2 * i
            tick(t, 0, False)
            tick(t + 1, 1, False)
            return carry

        lax.fori_loop(0, (n_full - 2) // 2, pair, 0)
        tick(n_full, 0, True)
        tick(n_full + 1, 1, True)

    values(n_blocks - 2, 0)
    softmax(1)
    values(n_blocks - 1, 1)
    o_ref[...] = (acc_s[...] / l_s[...]).T.astype(o_ref.dtype)


def _attention(q, k, vt):
    return pl.pallas_call(
        _attn_kernel,
        grid=(N_HEADS, SEQ // TQ),
        in_specs=[
            pl.BlockSpec((1, TQ, HEAD_PAD), lambda h, i: (h, i, 0)),
            pl.BlockSpec((1, SEQ, HEAD_PAD), lambda h, i: (h, 0, 0)),
            pl.BlockSpec((SEQ // TK, V_HEAD, TK), lambda h, i: (0, h, 0)),
        ],
        out_specs=pl.BlockSpec((TQ, V_HEAD), lambda h, i: (i, h)),
        out_shape=jax.ShapeDtypeStruct((SEQ, N_HEADS * V_HEAD), F32),
        scratch_shapes=[
            pltpu.VMEM((2, TK, TQ), F32),
            pltpu.VMEM((2, TK, TQ), BF16),
            pltpu.VMEM((2, 1, TQ), F32),
            pltpu.VMEM((2, 1, TQ), F32),
            pltpu.VMEM((1, TQ), F32),
            pltpu.VMEM((1, TQ), F32),
            pltpu.VMEM((V_HEAD, TQ), F32),
        ],
        compiler_params=_params("parallel", "arbitrary"),
        name="attn",
    )(q, k, vt)


def _mla_out_kernel(h_ref, o_ref, wgm_ref, wo_ref, a_ref, og_ref):
    h = h_ref[...]
    for c in range(D_MODEL // NCH):
        cols = slice(c * NCH, (c + 1) * NCH)
        g = _dot(h, wgm_ref[:, cols])
        og_ref[:, cols] = (o_ref[:, cols] * _silu(g)).astype(BF16)
    og = og_ref[...]
    for c in range(D_MODEL // NCH):
        cols = slice(c * NCH, (c + 1) * NCH)
        mm = _dot(h, wgm_ref[:, D_MODEL + c * NCH:D_MODEL + (c + 1) * NCH])
        a_ref[:, cols] = _sigmoid(mm) * _dot(og, wo_ref[:, cols])


def _mla_out(h, o, wl):
    return pl.pallas_call(
        _mla_out_kernel,
        grid=(SEQ // TM,),
        in_specs=[
            pl.BlockSpec((TM, D_MODEL), lambda i: (i, 0)),
            pl.BlockSpec((TM, D_MODEL), lambda i: (i, 0)),
            _resident((D_MODEL, 2 * D_MODEL)),
            _resident((D_MODEL, D_MODEL)),
        ],
        out_specs=pl.BlockSpec((TM, D_MODEL), lambda i: (i, 0)),
        out_shape=jax.ShapeDtypeStruct((SEQ, D_MODEL), F32),
        scratch_shapes=[pltpu.VMEM((TM, D_MODEL), BF16)],
        compiler_params=_params("parallel"),
        name="mla_out",
    )(h, o, wl["w_gm_mm"], wl["w_o_mla"])


def _lru_kernel(h_ref, win_ref, cw_ref, cb_ref, wg_ref, brg_ref, big_ref, lam_ref, wo_ref,
                b_out_ref, ubuf, a_s, b_s, hg_s, hcar):
    i = pl.program_id(0)

    @pl.when(i == 0)
    def _():
        ubuf[0:SUBLANES, :] = jnp.zeros((SUBLANES, D_MODEL), F32)
        hcar[...] = jnp.zeros_like(hcar)

    h = h_ref[...]
    for c in range(D_MODEL // NCH):
        cols = slice(c * NCH, (c + 1) * NCH)
        ubuf[SUBLANES:SUBLANES + TM, cols] = _dot(h, win_ref[:, cols])

    nl = -lam_ref[...]
    softplus = jnp.maximum(nl, 0.0) + jnp.log1p(jnp.exp(-jnp.abs(nl)))

    for j in range(LRU_BLOCKS):
        cols = slice(j * LRU_BLOCK_DIM, (j + 1) * LRU_BLOCK_DIM)
        xc = cb_ref[:, cols] + cw_ref[CONV_K - 1:CONV_K, cols] * ubuf[SUBLANES:SUBLANES + TM, cols]
        for kk in range(CONV_K - 1):
            off = SUBLANES - (CONV_K - 1) + kk
            xc = xc + cw_ref[kk:kk + 1, cols] * ubuf[off:off + TM, cols]
        ri = _dot(xc.astype(BF16), wg_ref[j])
        r = _sigmoid(ri[:, 0:LRU_BLOCK_DIM] + brg_ref[:, cols])
        ig = _sigmoid(ri[:, LRU_BLOCK_DIM:] + big_ref[:, cols])
        log_a = (-LRU_C * r) * softplus[:, cols]
        a = jnp.exp(log_a)
        a_s[:, cols] = a
        b_s[:, cols] = jnp.sqrt(-jnp.tanh(log_a) * (a * a + 1.0)) * (ig * xc)

    ubuf[0:SUBLANES, :] = ubuf[TM:TM + SUBLANES, :]

    SW = 512
    row = lax.broadcasted_iota(jnp.int32, (SUBLANES, SW), 0)
    for c in range(D_MODEL // SW):
        cols = slice(c * SW, (c + 1) * SW)

        def scan_body(rg, hprev, cols=cols):
            rows = pl.ds(pl.multiple_of(rg * SUBLANES, SUBLANES), SUBLANES)
            a8 = a_s[rows, cols]
            b8 = b_s[rows, cols]
            for d in (1, 2, 4):
                keep = row >= d
                b8 = jnp.where(keep, a8 * pltpu.roll(b8, d, 0) + b8, b8)
                a8 = jnp.where(keep, a8 * pltpu.roll(a8, d, 0), a8)
            hs = a8 * hprev + b8
            b_s[rows, cols] = hs
            return jnp.broadcast_to(hs[SUBLANES - 1:SUBLANES, :], (SUBLANES, SW))

        hcar[:, cols] = lax.fori_loop(0, TM // SUBLANES, scan_body, hcar[:, cols])

    for c in range(D_MODEL // NCH):
        cols = slice(c * NCH, (c + 1) * NCH)
        g = _dot(h, win_ref[:, D_MODEL + c * NCH:D_MODEL + (c + 1) * NCH])
        hg_s[:, cols] = (b_s[:, cols] * _silu(g)).astype(BF16)
    hg = hg_s[...]
    for c in range(D_MODEL // NCH):
        cols = slice(c * NCH, (c + 1) * NCH)
        ml = _dot(h, win_ref[:, 2 * D_MODEL + c * NCH:2 * D_MODEL + (c + 1) * NCH])
        b_out_ref[:, cols] = _sigmoid(ml) * _dot(hg, wo_ref[:, cols])


def _lru(h, wl):
    return pl.pallas_call(
        _lru_kernel,
        grid=(SEQ // TM,),
        in_specs=[
            pl.BlockSpec((TM, D_MODEL), lambda i: (i, 0)),
            _resident((D_MODEL, 3 * D_MODEL)),
            _resident((CONV_K, D_MODEL)), _resident((1, D_MODEL)),
            _resident((LRU_BLOCKS, LRU_BLOCK_DIM, 2 * LRU_BLOCK_DIM)),
            _resident((1, D_MODEL)), _resident((1, D_MODEL)), _resident((1, D_MODEL)),
            _resident((D_MODEL, D_MODEL)),
        ],
        out_specs=pl.BlockSpec((TM, D_MODEL), lambda i: (i, 0)),
        out_shape=jax.ShapeDtypeStruct((SEQ, D_MODEL), F32),
        scratch_shapes=[
            pltpu.VMEM((TM + SUBLANES, D_MODEL), F32),
            pltpu.VMEM((TM, D_MODEL), F32),
            pltpu.VMEM((TM, D_MODEL), F32),
            pltpu.VMEM((TM, D_MODEL), BF16),
            pltpu.VMEM((SUBLANES, D_MODEL), F32),
        ],
        compiler_params=_params("arbitrary"),
        name="lru",
    )(h, wl["w_lru_in"], wl["conv_w"], wl["conv_b"], wl["w_gates"], wl["b_rg"], wl["b_ig"],
      wl["lru_lambda"], wl["w_o_lru"])


def _final_kernel(x_ref, a_ref, b_ref, p_ref, wout_ref, pg_ref, wpg_ref, wple_ref, ng_ref,
                  *out_refs, last):
    merged = (a_ref[...] + b_ref[...]).astype(BF16)
    x1 = x_ref[...] + _dot(merged, wout_ref[...])
    n1 = _rms(x1, pg_ref[...]).astype(BF16)
    gate = _sigmoid(_dot(n1, wpg_ref[...]))
    x2 = x1 + _dot(p_ref[...].astype(BF16), wple_ref[...]) * gate
    normed = _rms(x2, ng_ref[...])
    if last:
        out_refs[0][...] = normed
    else:
        out_refs[0][...] = x2
        out_refs[1][...] = normed.astype(BF16)


def _final(x, a, b, p, layer, wl, next_gain, last):
    row = pl.BlockSpec((TM, D_MODEL), lambda i: (i, 0))
    if last:
        out_specs = [row]
        out_shape = [jax.ShapeDtypeStruct((SEQ, D_MODEL), F32)]
    else:
        out_specs = [row, row]
        out_shape = [jax.ShapeDtypeStruct((SEQ, D_MODEL), F32),
                     jax.ShapeDtypeStruct((SEQ, D_MODEL), BF16)]
    return pl.pallas_call(
        functools.partial(_final_kernel, last=last),
        grid=(SEQ // TM,),
        in_specs=[
            row, row, row,
            pl.BlockSpec((None, None, TM, PLE_DIM), lambda i: (layer, 0, i, 0)),
            _resident((D_MODEL, D_MODEL)), _resident((1, D_MODEL)),
            _resident((D_MODEL, D_MODEL)), _resident((PLE_DIM, D_MODEL)),
            _resident((1, D_MODEL)),
        ],
        out_specs=out_specs,
        out_shape=out_shape,
        compiler_params=_params("parallel"),
        name="final",
    )(x, a, b, p, wl["w_out"], wl["ple_norm"], wl["w_ple_gate"], wl["w_ple"], next_gain)


def _swap_halves(w):
    half = QK_ROPE // 2
    return jnp.concatenate([w[..., half:], w[..., :half]], axis=-1)


def _layer_weights(l, w_in, q_a_norm, w_q_b, kv_a_norm, w_kv_b, conv_w, conv_b, w_rg, b_rg,
                   w_ig, b_ig, lru_lambda, w_o_mla, w_o_lru, w_out, ple_norm, w_ple_gate, w_ple):
    wi = w_in[l]
    w_kr = wi[:, O_KR:O_GMLA]
    wq = w_q_b[l].reshape(Q_LORA, N_HEADS, QK_NOPE + QK_ROPE)
    wq_r = wq[:, :, QK_NOPE:]
    wkv = w_kv_b[l].reshape(KV_LORA, N_HEADS, QK_NOPE + V_HEAD)
    row = lambda v: v[l].reshape(1, -1)
    return {
        "w_lat": jnp.concatenate([wi[:, :O_KR], w_kr, _swap_halves(w_kr)], axis=1).astype(BF16),
        "q_a_norm": row(q_a_norm), "kv_a_norm": row(kv_a_norm),
        "w_qn": wq[:, :, :QK_NOPE].reshape(Q_LORA, D_MODEL).astype(BF16),
        "w_qr": jnp.concatenate([wq_r, _swap_halves(wq_r)], axis=-1).reshape(Q_LORA, D_MODEL).astype(BF16),
        "w_kn": wkv[:, :, :QK_NOPE].reshape(KV_LORA, D_MODEL).astype(BF16),
        "w_vt": wkv[:, :, QK_NOPE:].reshape(KV_LORA, D_MODEL).T.astype(BF16),
        "w_gm_mm": jnp.concatenate([wi[:, O_GMLA:O_U], wi[:, O_MMLA:O_MLRU]], axis=1).astype(BF16),
        "w_o_mla": w_o_mla[l].astype(BF16),
        "w_lru_in": jnp.concatenate([wi[:, O_U:O_MMLA], wi[:, O_MLRU:O_END]], axis=1).astype(BF16),
        "conv_w": conv_w[l], "conv_b": row(conv_b),
        "w_gates": jnp.concatenate([w_rg[l], w_ig[l]], axis=-1).astype(BF16),
        "b_rg": row(b_rg), "b_ig": row(b_ig), "lru_lambda": row(lru_lambda),
        "w_o_lru": w_o_lru[l].astype(BF16),
        "w_out": w_out[l].astype(BF16), "ple_norm": row(ple_norm),
        "w_ple_gate": w_ple_gate[l].astype(BF16), "w_ple": w_ple[l].astype(BF16),
    }


def kernel(x, p, positions, attn_norm, w_in, q_a_norm, w_q_b, kv_a_norm, w_kv_b, conv_w, conv_b,
           w_rg, b_rg, w_ig, b_ig, lru_lambda, w_o_mla, w_o_lru, w_out, ple_norm, w_ple_gate,
           w_ple, final_norm):
    assert x.shape == (1, SEQ, D_MODEL) and p.shape == (DEPTH, 1, SEQ, PLE_DIM)
    cs = _rope_table(positions)
    xs = x.reshape(SEQ, D_MODEL)
    h = _first_norm(xs, attn_norm[0].reshape(1, D_MODEL))
    out = None
    for l in range(DEPTH):
        wl = _layer_weights(l, w_in, q_a_norm, w_q_b, kv_a_norm, w_kv_b, conv_w, conv_b, w_rg,
                            b_rg, w_ig, b_ig, lru_lambda, w_o_mla, w_o_lru, w_out, ple_norm,
                            w_ple_gate, w_ple)
        q, k, vt = _qkv(h, wl, cs)
        o = _attention(q, k, vt)
        a = _mla_out(h, o, wl)
        b = _lru(h, wl)
        last = l == DEPTH - 1
        gain = (final_norm if last else attn_norm[l + 1]).reshape(1, D_MODEL)
        res = _final(xs, a, b, p, l, wl, gain, last)
        if last:
            out = res[0]
        else:
            xs, h = res
    return out.reshape(1, SEQ, D_MODEL)
```

```python
import functools
import math

import jax
import jax.numpy as jnp
from jax import lax
from jax.experimental import pallas as pl
from jax.experimental.pallas import tpu as pltpu

F32 = jnp.float32
BF16 = jnp.bfloat16

D_MODEL = 2048
SEQ = 8192
DEPTH = 4
CHUNK = 64
N_HEADS = 16
Q_LORA = 512
KV_LORA = 512
QK_NOPE = 128
QK_ROPE = 64
V_HEAD = 128
ROPE_THETA = 10000.0
LRU_BLOCKS = 16
LRU_BLOCK_DIM = 128
CONV_K = 4
LRU_C = 8.0
PLE_DIM = 256
EPS = 1e-6

O_QLAT, O_KVLAT, O_KR, O_GMLA, O_U, O_GLRU, O_MMLA, O_MLRU, O_END = (
    0, 512, 1024, 1088, 3136, 5184, 7232, 9280, 11328)

LANES = 128
SUBLANES = 8
HEAD_PAD = 256
VMEM_LIMIT = 56 * 1024 * 1024

TM = 256
TQ = 1024
TK = 512
HB = 1
NCH = 512
W_ROWS = 128


def _resident(shape):
    return pl.BlockSpec(shape, lambda *_: (0,) * len(shape), pipeline_mode=pl.Buffered(1))


def _params(*sem):
    return pltpu.CompilerParams(dimension_semantics=sem, vmem_limit_bytes=VMEM_LIMIT)


def _sigmoid(x):
    return 1.0 / (1.0 + jnp.exp(-x))


def _silu(x):
    return x * _sigmoid(x)


def _rms(x, g):
    return x * lax.rsqrt(jnp.mean(x * x, axis=-1, keepdims=True) + EPS) * g


def _dot(a, b):
    return jnp.dot(a, b, preferred_element_type=F32)


def _hbm():
    return pl.BlockSpec(memory_space=pl.ANY)


def _drop_half_vreg(x, width):
    half = LANES // 2
    lane = lax.broadcasted_iota(jnp.int32, (x.shape[0], LANES), 1)
    rot = [pltpu.roll(x[:, j * LANES:(j + 1) * LANES], half, 1) for j in range(width // LANES + 1)]
    return jnp.concatenate([jnp.where(lane < half, rot[j], rot[j + 1])
                            for j in range(width // LANES)], axis=1)


def _stage_weight(src, layer, col0, width, dst, dst_col0, stage, sem):
    rows = dst.shape[0]
    c0 = col0 // LANES * LANES
    assert col0 - c0 in (0, LANES // 2) and width % LANES == 0 and rows % W_ROWS == 0
    win = width if col0 == c0 else width + LANES
    assert c0 + win <= src.shape[2] and win <= stage.shape[2]

    def copy(ci):
        return pltpu.make_async_copy(
            src.at[layer, pl.ds(ci * W_ROWS, W_ROWS), pl.ds(c0, win)],
            stage.at[ci % 2, :, pl.ds(0, win)], sem.at[ci % 2])

    n = rows // W_ROWS
    copy(0).start()
    for ci in range(n):
        if ci + 1 < n:
            copy(ci + 1).start()
        copy(ci).wait()
        x = stage[ci % 2, :, 0:win]
        if col0 != c0:
            x = _drop_half_vreg(x, width)
        dst[ci * W_ROWS:(ci + 1) * W_ROWS, dst_col0:dst_col0 + width] = x.astype(BF16)


def _stage_scratch():
    return [pltpu.VMEM((2, W_ROWS, D_MODEL + LANES), F32), pltpu.SemaphoreType.DMA((2,))]


def _rope_kernel(pos_ref, invf_ref, cs_ref):
    ang = pos_ref[...].astype(F32) * invf_ref[...]
    lane = lax.broadcasted_iota(jnp.int32, ang.shape, 1)
    c = jnp.cos(ang)
    s = jnp.sin(ang)
    cs_ref[...] = jnp.where(lane < 64, c, jnp.where(lane < 96, -s, s))


def _rope_table(positions):
    rows = 1024
    inv_freq = ROPE_THETA ** (-jnp.arange(0, QK_ROPE, 2, dtype=F32) / QK_ROPE)
    invf = jnp.tile(inv_freq, 4).reshape(1, LANES)
    pos = positions.reshape(SEQ, 1)
    return pl.pallas_call(
        _rope_kernel,
        grid=(SEQ // rows,),
        in_specs=[pl.BlockSpec((rows, 1), lambda i: (i, 0)), _resident((1, LANES))],
        out_specs=pl.BlockSpec((rows, LANES), lambda i: (i, 0)),
        out_shape=jax.ShapeDtypeStruct((SEQ, LANES), F32),
        compiler_params=_params("parallel"),
        name="rope_table",
    )(pos, invf)


def _norm_kernel(x_ref, g_ref, h_ref):
    h_ref[...] = _rms(x_ref[...], g_ref[...]).astype(h_ref.dtype)


def _first_norm(x, g):
    return pl.pallas_call(
        _norm_kernel,
        grid=(SEQ // TM,),
        in_specs=[pl.BlockSpec((TM, D_MODEL), lambda i: (i, 0)), _resident((1, D_MODEL))],
        out_specs=pl.BlockSpec((TM, D_MODEL), lambda i: (i, 0)),
        out_shape=jax.ShapeDtypeStruct((SEQ, D_MODEL), BF16),
        compiler_params=_params("parallel"),
        name="first_norm",
    )(x, g)


def _qkv_kernel(h_ref, wlat_ref, qg_ref, kvg_ref, wqn_ref, wqr_ref, wkn_ref, wvt_ref, cs_ref,
                q_ref, k_ref, vt_ref):
    scale = math.log2(math.e) / math.sqrt(QK_NOPE + QK_ROPE)
    lat = _dot(h_ref[...], wlat_ref[...])
    qn = _rms(lat[:, 0:Q_LORA], qg_ref[...]).astype(BF16)
    kvn = _rms(lat[:, Q_LORA:Q_LORA + KV_LORA], kvg_ref[...]).astype(BF16)
    cs = cs_ref[...]
    lane = lax.broadcasted_iota(jnp.int32, cs.shape, 1)
    kr = lat[:, 2 * Q_LORA:2 * Q_LORA + LANES] * cs
    kr = kr + pltpu.roll(kr, 64, 1)
    kr = jnp.where(lane < 64, kr, 0.0).astype(BF16)
    vt_ref[0] = lax.dot_general(wvt_ref[...], kvn, (((1,), (1,)), ((), ())),
                                preferred_element_type=F32).astype(BF16)
    for hp in range(N_HEADS // 2):
        cols = slice(hp * 256, (hp + 1) * 256)
        qnope = _dot(qn, wqn_ref[:, cols]) * scale
        qrope = _dot(qn, wqr_ref[:, cols])
        knope = _dot(kvn, wkn_ref[:, cols])
        for j in range(2):
            hd = 2 * hp + j
            sub = slice(j * LANES, (j + 1) * LANES)
            r = qrope[:, sub] * cs
            r = (r + pltpu.roll(r, 64, 1)) * scale
            q_ref[hd, :, 0:LANES] = qnope[:, sub].astype(BF16)
            q_ref[hd, :, LANES:HEAD_PAD] = r.astype(BF16)
            k_ref[hd, :, 0:LANES] = knope[:, sub].astype(BF16)
            k_ref[hd, :, LANES:HEAD_PAD] = kr


def _qkv(h, wl, cs):
    return pl.pallas_call(
        _qkv_kernel,
        grid=(SEQ // TK,),
        in_specs=[
            pl.BlockSpec((TK, D_MODEL), lambda i: (i, 0)),
            _resident((D_MODEL, 2 * Q_LORA + LANES)),
            _resident((1, Q_LORA)), _resident((1, KV_LORA)),
            _resident((Q_LORA, D_MODEL)), _resident((Q_LORA, D_MODEL)),
            _resident((KV_LORA, D_MODEL)), _resident((N_HEADS * V_HEAD, KV_LORA)),
            pl.BlockSpec((TK, LANES), lambda i: (i, 0)),
        ],
        out_specs=[
            pl.BlockSpec((N_HEADS, TK, HEAD_PAD), lambda i: (0, i, 0)),
            pl.BlockSpec((N_HEADS, TK, HEAD_PAD), lambda i: (0, i, 0)),
            pl.BlockSpec((1, N_HEADS * V_HEAD, TK), lambda i: (i, 0, 0)),
        ],
        out_shape=[
            jax.ShapeDtypeStruct((N_HEADS, SEQ, HEAD_PAD), BF16),
            jax.ShapeDtypeStruct((N_HEADS, SEQ, HEAD_PAD), BF16),
            jax.ShapeDtypeStruct((SEQ // TK, N_HEADS * V_HEAD, TK), BF16),
        ],
        compiler_params=_params("parallel"),
        name="qkv",
    )(h, wl["w_lat"], wl["q_a_norm"], wl["kv_a_norm"], wl["w_qn"], wl["w_qr"], wl["w_kn"],
      wl["w_vt"], cs)


def _attn_kernel(q_ref, k_ref, vt_ref, o_ref, s_buf, p_buf, alpha_s, mb_s, m_s, l_s, acc_s):
    qi = pl.program_id(1)
    n_full = qi * (TQ // TK)
    n_blocks = n_full + TQ // TK
    m_s[...] = jnp.full(m_s.shape, -jnp.inf, F32)
    l_s[...] = jnp.zeros(l_s.shape, F32)
    acc_s[...] = jnp.zeros(acc_s.shape, F32)

    def scores(t, par, masked):
        for hb in range(HB):
            kt = k_ref[hb, pl.ds(pl.multiple_of(t * TK, TK), TK), :]
            st = lax.dot_general(kt, q_ref[hb], (((1,), (1,)), ((), ())),
                                 preferred_element_type=F32)
            if masked:
                key = (t * TK + lax.broadcasted_iota(jnp.int32, st.shape, 0)) // CHUNK
                qry = (qi * TQ + lax.broadcasted_iota(jnp.int32, st.shape, 1)) // CHUNK
                st = jnp.where(key <= qry, st, -jnp.inf)
            s_buf[hb, par] = st
            mb_s[hb, par] = jnp.max(st, axis=0, keepdims=True)

    def softmax(par):
        for hb in range(HB):
            m_old = m_s[hb]
            m_new = jnp.maximum(m_old, mb_s[hb, par])
            alpha = jnp.exp2(m_old - m_new)
            alpha_s[hb, par] = alpha
            m_s[hb] = m_new
            m_rows = jnp.broadcast_to(m_new, (SUBLANES, TQ))
            lsum = jnp.zeros((SUBLANES, TQ), F32)
            for r in range(0, TK, 2 * SUBLANES):
                lo = jnp.exp2(s_buf[hb, par, r:r + SUBLANES, :] - m_rows)
                hi = jnp.exp2(s_buf[hb, par, r + SUBLANES:r + 2 * SUBLANES, :] - m_rows)
                lsum = lsum + (lo + hi)
                p_buf[hb, par, r:r + 2 * SUBLANES, :] = (
                    jnp.concatenate([lo, hi], axis=0).astype(BF16))
            l_s[hb] = alpha * l_s[hb] + jnp.sum(lsum, axis=0, keepdims=True)

    def values(t, par):
        for hb in range(HB):
            vt = vt_ref[t, hb * V_HEAD:(hb + 1) * V_HEAD, :]
            for c in range(0, TQ, NCH):
                acc_s[hb, :, c:c + NCH] = (alpha_s[hb, par, :, c:c + NCH] * acc_s[hb, :, c:c + NCH]
                                           + _dot(vt, p_buf[hb, par, :, c:c + NCH]))

    def tick(t, par, masked):
        values(t - 2, par)
        softmax(1 - par)
        scores(t, par, masked)

    scores(0, 0, True)
    softmax(0)
    scores(1, 1, True)

    @pl.when(n_full > 0)
    def _():
        def pair(i, carry):
            t = 2 + 2 * i
            tick(t, 0, False)
            tick(t + 1, 1, False)
            return carry

        lax.fori_loop(0, (n_full - 2) // 2, pair, 0)
        tick(n_full, 0, True)
        tick(n_full + 1, 1, True)

    values(n_blocks - 2, 0)
    softmax(1)
    values(n_blocks - 1, 1)
    for hb in range(HB):
        o_ref[:, hb * V_HEAD:(hb + 1) * V_HEAD] = (acc_s[hb] / l_s[hb]).T.astype(o_ref.dtype)


def _attention(q, k, vt):
    return pl.pallas_call(
        _attn_kernel,
        grid=(N_HEADS // HB, SEQ // TQ),
        in_specs=[
            pl.BlockSpec((HB, TQ, HEAD_PAD), lambda h, i: (h, i, 0)),
            pl.BlockSpec((HB, SEQ, HEAD_PAD), lambda h, i: (h, 0, 0)),
            pl.BlockSpec((SEQ // TK, HB * V_HEAD, TK), lambda h, i: (0, h, 0)),
        ],
        out_specs=pl.BlockSpec((TQ, HB * V_HEAD), lambda h, i: (i, h)),
        out_shape=jax.ShapeDtypeStruct((SEQ, N_HEADS * V_HEAD), F32),
        scratch_shapes=[
            pltpu.VMEM((HB, 2, TK, TQ), F32),
            pltpu.VMEM((HB, 2, TK, TQ), BF16),
            pltpu.VMEM((HB, 2, 1, TQ), F32),
            pltpu.VMEM((HB, 2, 1, TQ), F32),
            pltpu.VMEM((HB, 1, TQ), F32),
            pltpu.VMEM((HB, 1, TQ), F32),
            pltpu.VMEM((HB, V_HEAD, TQ), F32),
        ],
        compiler_params=_params("parallel", "arbitrary"),
        name="attn",
    )(q, k, vt)


def _mla_out_kernel(h_ref, o_ref, win_hbm, wo_hbm, a_ref, wgm_ref, wo_ref, og_ref, stage, sem,
                    *, layer):
    @pl.when(pl.program_id(0) == 0)
    def _():
        _stage_weight(win_hbm, layer, O_GMLA, D_MODEL, wgm_ref, 0, stage, sem)
        _stage_weight(win_hbm, layer, O_MMLA, D_MODEL, wgm_ref, D_MODEL, stage, sem)
        _stage_weight(wo_hbm, layer, 0, D_MODEL, wo_ref, 0, stage, sem)

    h = h_ref[...]
    for c in range(D_MODEL // NCH):
        cols = slice(c * NCH, (c + 1) * NCH)
        g = _dot(h, wgm_ref[:, cols])
        og_ref[:, cols] = (o_ref[:, cols] * _silu(g)).astype(BF16)
    og = og_ref[...]
    for c in range(D_MODEL // NCH):
        cols = slice(c * NCH, (c + 1) * NCH)
        mm = _dot(h, wgm_ref[:, D_MODEL + c * NCH:D_MODEL + (c + 1) * NCH])
        a_ref[:, cols] = _sigmoid(mm) * _dot(og, wo_ref[:, cols])


def _mla_out(h, o, w_in, w_o_mla, layer):
    return pl.pallas_call(
        functools.partial(_mla_out_kernel, layer=layer),
        grid=(SEQ // TM,),
        in_specs=[
            pl.BlockSpec((TM, D_MODEL), lambda i: (i, 0)),
            pl.BlockSpec((TM, D_MODEL), lambda i: (i, 0)),
            _hbm(), _hbm(),
        ],
        out_specs=pl.BlockSpec((TM, D_MODEL), lambda i: (i, 0)),
        out_shape=jax.ShapeDtypeStruct((SEQ, D_MODEL), F32),
        scratch_shapes=[
            pltpu.VMEM((D_MODEL, 2 * D_MODEL), BF16),
            pltpu.VMEM((D_MODEL, D_MODEL), BF16),
            pltpu.VMEM((TM, D_MODEL), BF16),
        ] + _stage_scratch(),
        compiler_params=_params("arbitrary"),
        name="mla_out",
    )(h, o, w_in, w_o_mla)


def _lru_kernel(h_ref, win_hbm, wml_ref, cw_ref, cb_ref, wg_ref, brg_ref, big_ref, lam_ref, wo_hbm,
                b_out_ref, win_ref, wo_ref, ubuf, a_s, b_s, hg_s, ml_s, hcar, stage, sem, *, layer):
    i = pl.program_id(0)

    @pl.when(i == 0)
    def _():
        _stage_weight(win_hbm, layer, O_U, D_MODEL, win_ref, 0, stage, sem)
        _stage_weight(win_hbm, layer, O_GLRU, D_MODEL, win_ref, D_MODEL, stage, sem)
        _stage_weight(wo_hbm, layer, 0, D_MODEL, wo_ref, 0, stage, sem)
        ubuf[0:SUBLANES, :] = jnp.zeros((SUBLANES, D_MODEL), F32)
        hcar[...] = jnp.zeros_like(hcar)

    h = h_ref[...]
    for c in range(D_MODEL // NCH):
        cols = slice(c * NCH, (c + 1) * NCH)
        ubuf[SUBLANES:SUBLANES + TM, cols] = _dot(h, win_ref[:, cols])

    nl = -lam_ref[...]
    softplus = jnp.maximum(nl, 0.0) + jnp.log1p(jnp.exp(-jnp.abs(nl)))

    g_per = LRU_BLOCKS // (D_MODEL // NCH)
    for j in range(LRU_BLOCKS):
        cols = slice(j * LRU_BLOCK_DIM, (j + 1) * LRU_BLOCK_DIM)
        xc = cb_ref[:, cols] + cw_ref[CONV_K - 1:CONV_K, cols] * ubuf[SUBLANES:SUBLANES + TM, cols]
        for kk in range(CONV_K - 1):
            off = SUBLANES - (CONV_K - 1) + kk
            xc = xc + cw_ref[kk:kk + 1, cols] * ubuf[off:off + TM, cols]
        ri = _dot(xc.astype(BF16), wg_ref[j])
        r = _sigmoid(ri[:, 0:LRU_BLOCK_DIM] + brg_ref[:, cols])
        ig = _sigmoid(ri[:, LRU_BLOCK_DIM:] + big_ref[:, cols])
        log_a = (-LRU_C * r) * softplus[:, cols]
        a = jnp.exp(log_a)
        a_s[:, cols] = a
        b_s[:, cols] = jnp.sqrt(-jnp.tanh(log_a) * (a * a + 1.0)) * (ig * xc)
        if j % g_per == g_per - 1:
            c = j // g_per
            g = _dot(h, win_ref[:, D_MODEL + c * NCH:D_MODEL + (c + 1) * NCH])
            hg_s[:, c * NCH:(c + 1) * NCH] = _silu(g).astype(BF16)

    ubuf[0:SUBLANES, :] = ubuf[TM:TM + SUBLANES, :]

    SW = 512
    n_rg = TM // SUBLANES
    row = lax.broadcasted_iota(jnp.int32, (SUBLANES, SW), 0)
    keep = {d: row >= d for d in (1, 2, 4)}
    hprev = [hcar[:, c * SW:(c + 1) * SW] for c in range(D_MODEL // SW)]
    ml_every = n_rg // (D_MODEL // NCH)
    for rg in range(n_rg):
        rows = slice(rg * SUBLANES, (rg + 1) * SUBLANES)
        for c in range(D_MODEL // SW):
            cols = slice(c * SW, (c + 1) * SW)
            a8 = a_s[rows, cols]
            b8 = b_s[rows, cols]
            for d in (1, 2, 4):
                b8 = jnp.where(keep[d], a8 * pltpu.roll(b8, d, 0) + b8, b8)
                a8 = jnp.where(keep[d], a8 * pltpu.roll(a8, d, 0), a8)
            hs = a8 * hprev[c] + b8
            b_s[rows, cols] = hs
            hprev[c] = jnp.broadcast_to(hs[SUBLANES - 1:SUBLANES, :], (SUBLANES, SW))
        if rg % ml_every == ml_every - 1:
            c = rg // ml_every
            ml = _dot(h, wml_ref[:, c * NCH:(c + 1) * NCH])
            ml_s[:, c * NCH:(c + 1) * NCH] = _sigmoid(ml)
    for c in range(D_MODEL // SW):
        hcar[:, c * SW:(c + 1) * SW] = hprev[c]

    for c in range(D_MODEL // NCH):
        cols = slice(c * NCH, (c + 1) * NCH)
        hg_s[:, cols] = (b_s[:, cols] * hg_s[:, cols].astype(F32)).astype(BF16)
    hg = hg_s[...]
    for c in range(D_MODEL // NCH):
        cols = slice(c * NCH, (c + 1) * NCH)
        b_out_ref[:, cols] = ml_s[:, cols] * _dot(hg, wo_ref[:, cols])


def _lru(h, w_in, w_o_lru, wl, layer):
    return pl.pallas_call(
        functools.partial(_lru_kernel, layer=layer),
        grid=(SEQ // TM,),
        in_specs=[
            pl.BlockSpec((TM, D_MODEL), lambda i: (i, 0)),
            _hbm(),
            _resident((D_MODEL, D_MODEL)),
            _resident((CONV_K, D_MODEL)), _resident((1, D_MODEL)),
            _resident((LRU_BLOCKS, LRU_BLOCK_DIM, 2 * LRU_BLOCK_DIM)),
            _resident((1, D_MODEL)), _resident((1, D_MODEL)), _resident((1, D_MODEL)),
            _hbm(),
        ],
        out_specs=pl.BlockSpec((TM, D_MODEL), lambda i: (i, 0)),
        out_shape=jax.ShapeDtypeStruct((SEQ, D_MODEL), F32),
        scratch_shapes=[
            pltpu.VMEM((D_MODEL, 2 * D_MODEL), BF16),
            pltpu.VMEM((D_MODEL, D_MODEL), BF16),
            pltpu.VMEM((TM + SUBLANES, D_MODEL), F32),
            pltpu.VMEM((TM, D_MODEL), F32),
            pltpu.VMEM((TM, D_MODEL), F32),
            pltpu.VMEM((TM, D_MODEL), BF16),
            pltpu.VMEM((TM, D_MODEL), F32),
            pltpu.VMEM((SUBLANES, D_MODEL), F32),
        ] + _stage_scratch(),
        compiler_params=_params("arbitrary"),
        name="lru",
    )(h, w_in, wl["w_m_lru"], wl["conv_w"], wl["conv_b"], wl["w_gates"], wl["b_rg"], wl["b_ig"],
      wl["lru_lambda"], w_o_lru)


def _final_kernel(x_ref, a_ref, b_ref, p_ref, wout_hbm, pg_ref, wpg_hbm, wple_hbm, ng_ref,
                  *refs, layer, last):
    n_out = 1 if last else 2
    out_refs = refs[:n_out]
    wout_ref, wpg_ref, wple_ref, stage, sem = refs[n_out:]

    @pl.when(pl.program_id(0) == 0)
    def _():
        _stage_weight(wout_hbm, layer, 0, D_MODEL, wout_ref, 0, stage, sem)
        _stage_weight(wpg_hbm, layer, 0, D_MODEL, wpg_ref, 0, stage, sem)
        _stage_weight(wple_hbm, layer, 0, D_MODEL, wple_ref, 0, stage, sem)

    merged = (a_ref[...] + b_ref[...]).astype(BF16)
    x1 = x_ref[...] + _dot(merged, wout_ref[...])
    n1 = _rms(x1, pg_ref[...]).astype(BF16)
    gate = _sigmoid(_dot(n1, wpg_ref[...]))
    x2 = x1 + _dot(p_ref[...].astype(BF16), wple_ref[...]) * gate
    normed = _rms(x2, ng_ref[...])
    if last:
        out_refs[0][...] = normed
    else:
        out_refs[0][...] = x2
        out_refs[1][...] = normed.astype(BF16)


def _final(x, a, b, p, w_out, ple_gain, w_ple_gate, w_ple, next_gain, layer, last):
    row = pl.BlockSpec((TM, D_MODEL), lambda i: (i, 0))
    if last:
        out_specs = [row]
        out_shape = [jax.ShapeDtypeStruct((SEQ, D_MODEL), F32)]
    else:
        out_specs = [row, row]
        out_shape = [jax.ShapeDtypeStruct((SEQ, D_MODEL), F32),
                     jax.ShapeDtypeStruct((SEQ, D_MODEL), BF16)]
    return pl.pallas_call(
        functools.partial(_final_kernel, layer=layer, last=last),
        grid=(SEQ // TM,),
        in_specs=[
            row, row, row,
            pl.BlockSpec((None, None, TM, PLE_DIM), lambda i: (layer, 0, i, 0)),
            _hbm(), _resident((1, D_MODEL)), _hbm(), _hbm(), _resident((1, D_MODEL)),
        ],
        out_specs=out_specs,
        out_shape=out_shape,
        scratch_shapes=[
            pltpu.VMEM((D_MODEL, D_MODEL), BF16),
            pltpu.VMEM((D_MODEL, D_MODEL), BF16),
            pltpu.VMEM((PLE_DIM, D_MODEL), BF16),
        ] + _stage_scratch(),
        compiler_params=_params("arbitrary"),
        name="final",
    )(x, a, b, p, w_out, ple_gain, w_ple_gate, w_ple, next_gain)


def _swap_halves(w):
    half = QK_ROPE // 2
    return jnp.concatenate([w[..., half:], w[..., :half]], axis=-1)


def _layer_weights(l, w_in, q_a_norm, w_q_b, kv_a_norm, w_kv_b, conv_w, conv_b, w_rg, b_rg,
                   w_ig, b_ig, lru_lambda, ple_norm):
    wi = w_in[l]
    w_kr = wi[:, O_KR:O_GMLA]
    wq = w_q_b[l].reshape(Q_LORA, N_HEADS, QK_NOPE + QK_ROPE)
    wq_r = wq[:, :, QK_NOPE:]
    wkv = w_kv_b[l].reshape(KV_LORA, N_HEADS, QK_NOPE + V_HEAD)
    row = lambda v: v[l].reshape(1, -1)
    return {
        "w_lat": jnp.concatenate([wi[:, :O_KR], w_kr, _swap_halves(w_kr)], axis=1).astype(BF16),
        "q_a_norm": row(q_a_norm), "kv_a_norm": row(kv_a_norm),
        "w_qn": wq[:, :, :QK_NOPE].reshape(Q_LORA, D_MODEL).astype(BF16),
        "w_qr": jnp.concatenate([wq_r, _swap_halves(wq_r)], axis=-1).reshape(Q_LORA, D_MODEL).astype(BF16),
        "w_kn": wkv[:, :, :QK_NOPE].reshape(KV_LORA, D_MODEL).astype(BF16),
        "w_vt": wkv[:, :, QK_NOPE:].reshape(KV_LORA, D_MODEL).T.astype(BF16),
        "w_m_lru": wi[:, O_MLRU:O_END].astype(BF16),
        "conv_w": conv_w[l], "conv_b": row(conv_b),
        "w_gates": jnp.concatenate([w_rg[l], w_ig[l]], axis=-1).astype(BF16),
        "b_rg": row(b_rg), "b_ig": row(b_ig), "lru_lambda": row(lru_lambda),
        "ple_norm": row(ple_norm),
    }


def kernel(x, p, positions, attn_norm, w_in, q_a_norm, w_q_b, kv_a_norm, w_kv_b, conv_w, conv_b,
           w_rg, b_rg, w_ig, b_ig, lru_lambda, w_o_mla, w_o_lru, w_out, ple_norm, w_ple_gate,
           w_ple, final_norm):
    assert x.shape == (1, SEQ, D_MODEL) and p.shape == (DEPTH, 1, SEQ, PLE_DIM)
    cs = _rope_table(positions)
    xs = x.reshape(SEQ, D_MODEL)
    h = _first_norm(xs, attn_norm[0].reshape(1, D_MODEL))
    out = None
    for l in range(DEPTH):
        wl = _layer_weights(l, w_in, q_a_norm, w_q_b, kv_a_norm, w_kv_b, conv_w, conv_b, w_rg,
                            b_rg, w_ig, b_ig, lru_lambda, ple_norm)
        q, k, vt = _qkv(h, wl, cs)
        o = _attention(q, k, vt)
        a = _mla_out(h, o, w_in, w_o_mla, l)
        b = _lru(h, w_in, w_o_lru, wl, l)
        last = l == DEPTH - 1
        gain = (final_norm if last else attn_norm[l + 1]).reshape(1, D_MODEL)
        res = _final(xs, a, b, p, w_out, wl["ple_norm"], w_ple_gate, w_ple, gain, l, last)
        if last:
            out = res[0]
        else:
            xs, h = res
    return out.reshape(1, SEQ, D_MODEL)
```

```python
import functools
import math

import jax
import jax.numpy as jnp
from jax import lax
from jax.experimental import pallas as pl
from jax.experimental.pallas import tpu as pltpu

F32 = jnp.float32
BF16 = jnp.bfloat16

D_MODEL = 2048
SEQ = 8192
DEPTH = 4
CHUNK = 64
N_HEADS = 16
Q_LORA = 512
KV_LORA = 512
QK_NOPE = 128
QK_ROPE = 64
V_HEAD = 128
ROPE_THETA = 10000.0
LRU_BLOCKS = 16
LRU_BLOCK_DIM = 128
CONV_K = 4
LRU_C = 8.0
PLE_DIM = 256
EPS = 1e-6

O_QLAT, O_KVLAT, O_KR, O_GMLA, O_U, O_GLRU, O_MMLA, O_MLRU, O_END = (
    0, 512, 1024, 1088, 3136, 5184, 7232, 9280, 11328)

LANES = 128
SUBLANES = 8
HEAD_PAD = 256
VMEM_LIMIT = 56 * 1024 * 1024

TM = 256
TQ = 1024
TK = 512
HB = 1
NCH = 512
W_ROWS = 128


def _resident(shape):
    return pl.BlockSpec(shape, lambda *_: (0,) * len(shape), pipeline_mode=pl.Buffered(1))


def _params(*sem):
    return pltpu.CompilerParams(dimension_semantics=sem, vmem_limit_bytes=VMEM_LIMIT)


def _sigmoid(x):
    return 1.0 / (1.0 + jnp.exp(-x))


def _silu(x):
    return x * _sigmoid(x)


def _rms(x, g):
    return x * lax.rsqrt(jnp.mean(x * x, axis=-1, keepdims=True) + EPS) * g


def _dot(a, b):
    return jnp.dot(a, b, preferred_element_type=F32)


def _hbm():
    return pl.BlockSpec(memory_space=pl.ANY)


def _stage_weight(src, layer, dst, stage, sem):
    rows, width = dst.shape
    assert rows % W_ROWS == 0 and width <= stage.shape[2]

    def copy(ci):
        return pltpu.make_async_copy(
            src.at[layer, pl.ds(ci * W_ROWS, W_ROWS), :],
            stage.at[ci % 2, :, pl.ds(0, width)], sem.at[ci % 2])

    n = rows // W_ROWS
    copy(0).start()
    for ci in range(n):
        if ci + 1 < n:
            copy(ci + 1).start()
        copy(ci).wait()
        dst[ci * W_ROWS:(ci + 1) * W_ROWS, :] = stage[ci % 2, :, 0:width].astype(BF16)


def _stage_weight_t(src_t, layer, row0, width, dst, dst_col0, stage, sem):
    assert width % W_ROWS == 0 and row0 % SUBLANES == 0 and dst.shape[0] == stage.shape[2]

    def copy(ci):
        return pltpu.make_async_copy(
            src_t.at[layer, pl.ds(row0 + ci * W_ROWS, W_ROWS), :], stage.at[ci % 2], sem.at[ci % 2])

    n = width // W_ROWS
    copy(0).start()
    for ci in range(n):
        if ci + 1 < n:
            copy(ci + 1).start()
        copy(ci).wait()
        cols = slice(dst_col0 + ci * W_ROWS, dst_col0 + (ci + 1) * W_ROWS)
        dst[:, cols] = stage[ci % 2].T.astype(BF16)


def _stage_scratch():
    return [pltpu.VMEM((2, W_ROWS, D_MODEL), F32), pltpu.SemaphoreType.DMA((2,))]


def _rope_kernel(pos_ref, invf_ref, cs_ref):
    ang = pos_ref[...].astype(F32) * invf_ref[...]
    lane = lax.broadcasted_iota(jnp.int32, ang.shape, 1)
    c = jnp.cos(ang)
    s = jnp.sin(ang)
    cs_ref[...] = jnp.where(lane < 64, c, jnp.where(lane < 96, -s, s))


def _rope_table(positions):
    rows = 1024
    inv_freq = ROPE_THETA ** (-jnp.arange(0, QK_ROPE, 2, dtype=F32) / QK_ROPE)
    invf = jnp.tile(inv_freq, 4).reshape(1, LANES)
    pos = positions.reshape(SEQ, 1)
    return pl.pallas_call(
        _rope_kernel,
        grid=(SEQ // rows,),
        in_specs=[pl.BlockSpec((rows, 1), lambda i: (i, 0)), _resident((1, LANES))],
        out_specs=pl.BlockSpec((rows, LANES), lambda i: (i, 0)),
        out_shape=jax.ShapeDtypeStruct((SEQ, LANES), F32),
        compiler_params=_params("parallel"),
        name="rope_table",
    )(pos, invf)


def _norm_kernel(x_ref, g_ref, h_ref):
    h_ref[...] = _rms(x_ref[...], g_ref[...]).astype(h_ref.dtype)


def _first_norm(x, g):
    return pl.pallas_call(
        _norm_kernel,
        grid=(SEQ // TM,),
        in_specs=[pl.BlockSpec((TM, D_MODEL), lambda i: (i, 0)), _resident((1, D_MODEL))],
        out_specs=pl.BlockSpec((TM, D_MODEL), lambda i: (i, 0)),
        out_shape=jax.ShapeDtypeStruct((SEQ, D_MODEL), BF16),
        compiler_params=_params("parallel"),
        name="first_norm",
    )(x, g)


def _qkv_kernel(h_ref, win_hbm, qg_ref, kvg_ref, wqn_ref, wqr_ref, wkn_ref, wvt_ref, cs_ref,
                q_ref, k_ref, vt_ref, wlat_ref, stage, sem, *, layer):
    @pl.when(pl.program_id(0) == 0)
    def _():
        _stage_weight_t(win_hbm, layer, O_QLAT, 2 * Q_LORA, wlat_ref, 0, stage, sem)
        half = QK_ROPE // 2
        pieces = ((O_KR, 0, QK_ROPE), (O_KR + half, QK_ROPE, half), (O_KR, QK_ROPE + half, half))
        copies = [pltpu.make_async_copy(win_hbm.at[layer, pl.ds(r0, n), :],
                                        stage.at[0, pl.ds(d0, n), :], sem.at[0])
                  for r0, d0, n in pieces]
        for cp in copies:
            cp.start()
        for cp in copies:
            cp.wait()
        wlat_ref[:, 2 * Q_LORA:2 * Q_LORA + LANES] = stage[0].T.astype(BF16)

    scale = math.log2(math.e) / math.sqrt(QK_NOPE + QK_ROPE)
    lat = _dot(h_ref[...], wlat_ref[...])
    qn = _rms(lat[:, 0:Q_LORA], qg_ref[...]).astype(BF16)
    kvn = _rms(lat[:, Q_LORA:Q_LORA + KV_LORA], kvg_ref[...]).astype(BF16)
    cs = cs_ref[...]
    lane = lax.broadcasted_iota(jnp.int32, cs.shape, 1)
    kr = lat[:, 2 * Q_LORA:2 * Q_LORA + LANES] * cs
    kr = kr + pltpu.roll(kr, 64, 1)
    kr = jnp.where(lane < 64, kr, 0.0).astype(BF16)
    vt_ref[0] = lax.dot_general(wvt_ref[...], kvn, (((1,), (1,)), ((), ())),
                                preferred_element_type=F32).astype(BF16)
    for hp in range(N_HEADS // 2):
        cols = slice(hp * 256, (hp + 1) * 256)
        qnope = _dot(qn, wqn_ref[:, cols]) * scale
        qrope = _dot(qn, wqr_ref[:, cols])
        knope = _dot(kvn, wkn_ref[:, cols])
        for j in range(2):
            hd = 2 * hp + j
            sub = slice(j * LANES, (j + 1) * LANES)
            r = qrope[:, sub] * cs
            r = (r + pltpu.roll(r, 64, 1)) * scale
            q_ref[hd, :, 0:LANES] = qnope[:, sub].astype(BF16)
            q_ref[hd, :, LANES:HEAD_PAD] = r.astype(BF16)
            k_ref[hd, :, 0:LANES] = knope[:, sub].astype(BF16)
            k_ref[hd, :, LANES:HEAD_PAD] = kr


def _qkv(h, w_in, wl, cs, layer):
    return pl.pallas_call(
        functools.partial(_qkv_kernel, layer=layer),
        grid=(SEQ // TK,),
        in_specs=[
            pl.BlockSpec((TK, D_MODEL), lambda i: (i, 0)),
            _hbm(),
            _resident((1, Q_LORA)), _resident((1, KV_LORA)),
            _resident((Q_LORA, D_MODEL)), _resident((Q_LORA, D_MODEL)),
            _resident((KV_LORA, D_MODEL)), _resident((N_HEADS * V_HEAD, KV_LORA)),
            pl.BlockSpec((TK, LANES), lambda i: (i, 0)),
        ],
        out_specs=[
            pl.BlockSpec((N_HEADS, TK, HEAD_PAD), lambda i: (0, i, 0)),
            pl.BlockSpec((N_HEADS, TK, HEAD_PAD), lambda i: (0, i, 0)),
            pl.BlockSpec((1, N_HEADS * V_HEAD, TK), lambda i: (i, 0, 0)),
        ],
        out_shape=[
            jax.ShapeDtypeStruct((N_HEADS, SEQ, HEAD_PAD), BF16),
            jax.ShapeDtypeStruct((N_HEADS, SEQ, HEAD_PAD), BF16),
            jax.ShapeDtypeStruct((SEQ // TK, N_HEADS * V_HEAD, TK), BF16),
        ],
        scratch_shapes=[
            pltpu.VMEM((D_MODEL, 2 * Q_LORA + LANES), BF16),
        ] + _stage_scratch(),
        compiler_params=_params("arbitrary"),
        name="qkv",
    )(h, w_in, wl["q_a_norm"], wl["kv_a_norm"], wl["w_qn"], wl["w_qr"], wl["w_kn"],
      wl["w_vt"], cs)


def _attn_kernel(q_ref, k_ref, vt_ref, o_ref, s_buf, p_buf, alpha_s, mb_s, m_s, l_s, acc_s):
    qi = pl.program_id(1)
    n_full = qi * (TQ // TK)
    n_blocks = n_full + TQ // TK
    m_s[...] = jnp.full(m_s.shape, -jnp.inf, F32)
    l_s[...] = jnp.zeros(l_s.shape, F32)
    acc_s[...] = jnp.zeros(acc_s.shape, F32)

    def scores(t, par, masked):
        for hb in range(HB):
            kt = k_ref[hb, pl.ds(pl.multiple_of(t * TK, TK), TK), :]
            st = lax.dot_general(kt, q_ref[hb], (((1,), (1,)), ((), ())),
                                 preferred_element_type=F32)
            if masked:
                key = (t * TK + lax.broadcasted_iota(jnp.int32, st.shape, 0)) // CHUNK
                qry = (qi * TQ + lax.broadcasted_iota(jnp.int32, st.shape, 1)) // CHUNK
                st = jnp.where(key <= qry, st, -jnp.inf)
            s_buf[hb, par] = st
            mb_s[hb, par] = jnp.max(st, axis=0, keepdims=True)

    def softmax(par):
        for hb in range(HB):
            m_old = m_s[hb]
            m_new = jnp.maximum(m_old, mb_s[hb, par])
            alpha = jnp.exp2(m_old - m_new)
            alpha_s[hb, par] = alpha
            m_s[hb] = m_new
            m_rows = jnp.broadcast_to(m_new, (SUBLANES, TQ))
            lsum = jnp.zeros((SUBLANES, TQ), F32)
            for r in range(0, TK, 2 * SUBLANES):
                lo = jnp.exp2(s_buf[hb, par, r:r + SUBLANES, :] - m_rows)
                hi = jnp.exp2(s_buf[hb, par, r + SUBLANES:r + 2 * SUBLANES, :] - m_rows)
                lsum = lsum + (lo + hi)
                p_buf[hb, par, r:r + 2 * SUBLANES, :] = (
                    jnp.concatenate([lo, hi], axis=0).astype(BF16))
            l_s[hb] = alpha * l_s[hb] + jnp.sum(lsum, axis=0, keepdims=True)

    def values(t, par):
        for hb in range(HB):
            vt = vt_ref[t, hb * V_HEAD:(hb + 1) * V_HEAD, :]
            for c in range(0, TQ, NCH):
                acc_s[hb, :, c:c + NCH] = (alpha_s[hb, par, :, c:c + NCH] * acc_s[hb, :, c:c + NCH]
                                           + _dot(vt, p_buf[hb, par, :, c:c + NCH]))

    def tick(t, par, masked):
        values(t - 2, par)
        softmax(1 - par)
        scores(t, par, masked)

    scores(0, 0, True)
    softmax(0)
    scores(1, 1, True)

    @pl.when(n_full > 0)
    def _():
        def pair(i, carry):
            t = 2 + 2 * i
            tick(t, 0, False)
            tick(t + 1, 1, False)
            return carry

        lax.fori_loop(0, (n_full - 2) // 2, pair, 0)
        tick(n_full, 0, True)
        tick(n_full + 1, 1, True)

    values(n_blocks - 2, 0)
    softmax(1)
    values(n_blocks - 1, 1)
    for hb in range(HB):
        o_ref[:, hb * V_HEAD:(hb + 1) * V_HEAD] = (acc_s[hb] / l_s[hb]).T.astype(o_ref.dtype)


def _attention(q, k, vt):
    return pl.pallas_call(
        _attn_kernel,
        grid=(N_HEADS // HB, SEQ // TQ),
        in_specs=[
            pl.BlockSpec((HB, TQ, HEAD_PAD), lambda h, i: (h, i, 0)),
            pl.BlockSpec((HB, SEQ, HEAD_PAD), lambda h, i: (h, 0, 0)),
            pl.BlockSpec((SEQ // TK, HB * V_HEAD, TK), lambda h, i: (0, h, 0)),
        ],
        out_specs=pl.BlockSpec((TQ, HB * V_HEAD), lambda h, i: (i, h)),
        out_shape=jax.ShapeDtypeStruct((SEQ, N_HEADS * V_HEAD), F32),
        scratch_shapes=[
            pltpu.VMEM((HB, 2, TK, TQ), F32),
            pltpu.VMEM((HB, 2, TK, TQ), BF16),
            pltpu.VMEM((HB, 2, 1, TQ), F32),
            pltpu.VMEM((HB, 2, 1, TQ), F32),
            pltpu.VMEM((HB, 1, TQ), F32),
            pltpu.VMEM((HB, 1, TQ), F32),
            pltpu.VMEM((HB, V_HEAD, TQ), F32),
        ],
        compiler_params=_params("parallel", "arbitrary"),
        name="attn",
    )(q, k, vt)


def _mla_out_kernel(h_ref, o_ref, win_hbm, wo_hbm, a_ref, wgm_ref, wo_ref, og_ref, stage, sem,
                    *, layer):
    @pl.when(pl.program_id(0) == 0)
    def _():
        _stage_weight_t(win_hbm, layer, O_GMLA, D_MODEL, wgm_ref, 0, stage, sem)
        _stage_weight_t(win_hbm, layer, O_MMLA, D_MODEL, wgm_ref, D_MODEL, stage, sem)
        _stage_weight(wo_hbm, layer, wo_ref, stage, sem)

    h = h_ref[...]
    for c in range(D_MODEL // NCH):
        cols = slice(c * NCH, (c + 1) * NCH)
        g = _dot(h, wgm_ref[:, cols])
        og_ref[:, cols] = (o_ref[:, cols] * _silu(g)).astype(BF16)
    og = og_ref[...]
    for c in range(D_MODEL // NCH):
        cols = slice(c * NCH, (c + 1) * NCH)
        mm = _dot(h, wgm_ref[:, D_MODEL + c * NCH:D_MODEL + (c + 1) * NCH])
        a_ref[:, cols] = _sigmoid(mm) * _dot(og, wo_ref[:, cols])


def _mla_out(h, o, w_in, w_o_mla, layer):
    return pl.pallas_call(
        functools.partial(_mla_out_kernel, layer=layer),
        grid=(SEQ // TM,),
        in_specs=[
            pl.BlockSpec((TM, D_MODEL), lambda i: (i, 0)),
            pl.BlockSpec((TM, D_MODEL), lambda i: (i, 0)),
            _hbm(), _hbm(),
        ],
        out_specs=pl.BlockSpec((TM, D_MODEL), lambda i: (i, 0)),
        out_shape=jax.ShapeDtypeStruct((SEQ, D_MODEL), F32),
        scratch_shapes=[
            pltpu.VMEM((D_MODEL, 2 * D_MODEL), BF16),
            pltpu.VMEM((D_MODEL, D_MODEL), BF16),
            pltpu.VMEM((TM, D_MODEL), BF16),
        ] + _stage_scratch(),
        compiler_params=_params("arbitrary"),
        name="mla_out",
    )(h, o, w_in, w_o_mla)


def _lru_kernel(h_ref, win_hbm, cw_ref, cb_ref, wg_ref, brg_ref, big_ref, lam_ref, wo_hbm,
                b_out_ref, win_ref, wo_ref, ubuf, a_s, b_s, hg_s, ml_s, hcar, stage, sem, *, layer):
    i = pl.program_id(0)

    @pl.when(i == 0)
    def _():
        _stage_weight_t(win_hbm, layer, O_U, D_MODEL, win_ref, 0, stage, sem)
        _stage_weight_t(win_hbm, layer, O_GLRU, D_MODEL, win_ref, D_MODEL, stage, sem)
        _stage_weight_t(win_hbm, layer, O_MLRU, D_MODEL, win_ref, 2 * D_MODEL, stage, sem)
        _stage_weight(wo_hbm, layer, wo_ref, stage, sem)
        ubuf[0:SUBLANES, :] = jnp.zeros((SUBLANES, D_MODEL), F32)
        hcar[...] = jnp.zeros_like(hcar)

    h = h_ref[...]
    for c in range(D_MODEL // NCH):
        cols = slice(c * NCH, (c + 1) * NCH)
        ubuf[SUBLANES:SUBLANES + TM, cols] = _dot(h, win_ref[:, cols])

    nl = -lam_ref[...]
    softplus = jnp.maximum(nl, 0.0) + jnp.log1p(jnp.exp(-jnp.abs(nl)))

    g_per = LRU_BLOCKS // (D_MODEL // NCH)
    for j in range(LRU_BLOCKS):
        cols = slice(j * LRU_BLOCK_DIM, (j + 1) * LRU_BLOCK_DIM)
        xc = cb_ref[:, cols] + cw_ref[CONV_K - 1:CONV_K, cols] * ubuf[SUBLANES:SUBLANES + TM, cols]
        for kk in range(CONV_K - 1):
            off = SUBLANES - (CONV_K - 1) + kk
            xc = xc + cw_ref[kk:kk + 1, cols] * ubuf[off:off + TM, cols]
        ri = _dot(xc.astype(BF16), wg_ref[j])
        r = _sigmoid(ri[:, 0:LRU_BLOCK_DIM] + brg_ref[:, cols])
        ig = _sigmoid(ri[:, LRU_BLOCK_DIM:] + big_ref[:, cols])
        log_a = (-LRU_C * r) * softplus[:, cols]
        a = jnp.exp(log_a)
        a_s[:, cols] = a
        b_s[:, cols] = jnp.sqrt(-jnp.tanh(log_a) * (a * a + 1.0)) * (ig * xc)
        if j % g_per == g_per - 1:
            c = j // g_per
            g = _dot(h, win_ref[:, D_MODEL + c * NCH:D_MODEL + (c + 1) * NCH])
            hg_s[:, c * NCH:(c + 1) * NCH] = _silu(g).astype(BF16)

    ubuf[0:SUBLANES, :] = ubuf[TM:TM + SUBLANES, :]

    SW = 512
    n_rg = TM // SUBLANES
    row = lax.broadcasted_iota(jnp.int32, (SUBLANES, SW), 0)
    keep = {d: row >= d for d in (1, 2, 4)}
    hprev = [hcar[:, c * SW:(c + 1) * SW] for c in range(D_MODEL // SW)]
    ml_every = n_rg // (D_MODEL // NCH)
    for rg in range(n_rg):
        rows = slice(rg * SUBLANES, (rg + 1) * SUBLANES)
        for c in range(D_MODEL // SW):
            cols = slice(c * SW, (c + 1) * SW)
            a8 = a_s[rows, cols]
            b8 = b_s[rows, cols]
            for d in (1, 2, 4):
                b8 = jnp.where(keep[d], a8 * pltpu.roll(b8, d, 0) + b8, b8)
                a8 = jnp.where(keep[d], a8 * pltpu.roll(a8, d, 0), a8)
            hs = a8 * hprev[c] + b8
            b_s[rows, cols] = hs
            hprev[c] = jnp.broadcast_to(hs[SUBLANES - 1:SUBLANES, :], (SUBLANES, SW))
        if rg % ml_every == ml_every - 1:
            c = rg // ml_every
            ml = _dot(h, win_ref[:, 2 * D_MODEL + c * NCH:2 * D_MODEL + (c + 1) * NCH])
            ml_s[:, c * NCH:(c + 1) * NCH] = _sigmoid(ml)
    for c in range(D_MODEL // SW):
        hcar[:, c * SW:(c + 1) * SW] = hprev[c]

    for c in range(D_MODEL // NCH):
        cols = slice(c * NCH, (c + 1) * NCH)
        hg_s[:, cols] = (b_s[:, cols] * hg_s[:, cols].astype(F32)).astype(BF16)
    hg = hg_s[...]
    for c in range(D_MODEL // NCH):
        cols = slice(c * NCH, (c + 1) * NCH)
        b_out_ref[:, cols] = ml_s[:, cols] * _dot(hg, wo_ref[:, cols])


def _lru(h, w_in, w_o_lru, wl, layer):
    return pl.pallas_call(
        functools.partial(_lru_kernel, layer=layer),
        grid=(SEQ // TM,),
        in_specs=[
            pl.BlockSpec((TM, D_MODEL), lambda i: (i, 0)),
            _hbm(),
            _resident((CONV_K, D_MODEL)), _resident((1, D_MODEL)),
            _resident((LRU_BLOCKS, LRU_BLOCK_DIM, 2 * LRU_BLOCK_DIM)),
            _resident((1, D_MODEL)), _resident((1, D_MODEL)), _resident((1, D_MODEL)),
            _hbm(),
        ],
        out_specs=pl.BlockSpec((TM, D_MODEL), lambda i: (i, 0)),
        out_shape=jax.ShapeDtypeStruct((SEQ, D_MODEL), F32),
        scratch_shapes=[
            pltpu.VMEM((D_MODEL, 3 * D_MODEL), BF16),
            pltpu.VMEM((D_MODEL, D_MODEL), BF16),
            pltpu.VMEM((TM + SUBLANES, D_MODEL), F32),
            pltpu.VMEM((TM, D_MODEL), F32),
            pltpu.VMEM((TM, D_MODEL), F32),
            pltpu.VMEM((TM, D_MODEL), BF16),
            pltpu.VMEM((TM, D_MODEL), F32),
            pltpu.VMEM((SUBLANES, D_MODEL), F32),
        ] + _stage_scratch(),
        compiler_params=_params("arbitrary"),
        name="lru",
    )(h, w_in, wl["conv_w"], wl["conv_b"], wl["w_gates"], wl["b_rg"], wl["b_ig"],
      wl["lru_lambda"], w_o_lru)


def _final_kernel(x_ref, a_ref, b_ref, p_ref, wout_hbm, pg_ref, wpg_hbm, wple_hbm, ng_ref,
                  *refs, layer, last):
    n_out = 1 if last else 2
    out_refs = refs[:n_out]
    wout_ref, wpg_ref, wple_ref, stage, sem = refs[n_out:]

    @pl.when(pl.program_id(0) == 0)
    def _():
        _stage_weight(wout_hbm, layer, wout_ref, stage, sem)
        _stage_weight(wpg_hbm, layer, wpg_ref, stage, sem)
        _stage_weight(wple_hbm, layer, wple_ref, stage, sem)

    merged = (a_ref[...] + b_ref[...]).astype(BF16)
    x1 = x_ref[...] + _dot(merged, wout_ref[...])
    n1 = _rms(x1, pg_ref[...]).astype(BF16)
    gate = _sigmoid(_dot(n1, wpg_ref[...]))
    x2 = x1 + _dot(p_ref[...].astype(BF16), wple_ref[...]) * gate
    normed = _rms(x2, ng_ref[...])
    if last:
        out_refs[0][...] = normed
    else:
        out_refs[0][...] = x2
        out_refs[1][...] = normed.astype(BF16)


def _final(x, a, b, p, w_out, ple_gain, w_ple_gate, w_ple, next_gain, layer, last):
    row = pl.BlockSpec((TM, D_MODEL), lambda i: (i, 0))
    if last:
        out_specs = [row]
        out_shape = [jax.ShapeDtypeStruct((SEQ, D_MODEL), F32)]
    else:
        out_specs = [row, row]
        out_shape = [jax.ShapeDtypeStruct((SEQ, D_MODEL), F32),
                     jax.ShapeDtypeStruct((SEQ, D_MODEL), BF16)]
    return pl.pallas_call(
        functools.partial(_final_kernel, layer=layer, last=last),
        grid=(SEQ // TM,),
        in_specs=[
            row, row, row,
            pl.BlockSpec((None, None, TM, PLE_DIM), lambda i: (layer, 0, i, 0)),
            _hbm(), _resident((1, D_MODEL)), _hbm(), _hbm(), _resident((1, D_MODEL)),
        ],
        out_specs=out_specs,
        out_shape=out_shape,
        scratch_shapes=[
            pltpu.VMEM((D_MODEL, D_MODEL), BF16),
            pltpu.VMEM((D_MODEL, D_MODEL), BF16),
            pltpu.VMEM((PLE_DIM, D_MODEL), BF16),
        ] + _stage_scratch(),
        compiler_params=_params("arbitrary"),
        name="final",
    )(x, a, b, p, w_out, ple_gain, w_ple_gate, w_ple, next_gain)


def _swap_halves(w):
    half = QK_ROPE // 2
    return jnp.concatenate([w[..., half:], w[..., :half]], axis=-1)


def _layer_weights(l, q_a_norm, w_q_b, kv_a_norm, w_kv_b, conv_w, conv_b, w_rg, b_rg,
                   w_ig, b_ig, lru_lambda, ple_norm):
    wq = w_q_b[l].reshape(Q_LORA, N_HEADS, QK_NOPE + QK_ROPE)
    wq_r = wq[:, :, QK_NOPE:]
    wkv = w_kv_b[l].reshape(KV_LORA, N_HEADS, QK_NOPE + V_HEAD)
    row = lambda v: v[l].reshape(1, -1)
    return {
        "q_a_norm": row(q_a_norm), "kv_a_norm": row(kv_a_norm),
        "w_qn": wq[:, :, :QK_NOPE].reshape(Q_LORA, D_MODEL).astype(BF16),
        "w_qr": jnp.concatenate([wq_r, _swap_halves(wq_r)], axis=-1).reshape(Q_LORA, D_MODEL).astype(BF16),
        "w_kn": wkv[:, :, :QK_NOPE].reshape(KV_LORA, D_MODEL).astype(BF16),
        "w_vt": wkv[:, :, QK_NOPE:].reshape(KV_LORA, D_MODEL).T.astype(BF16),
        "conv_w": conv_w[l], "conv_b": row(conv_b),
        "w_gates": jnp.concatenate([w_rg[l], w_ig[l]], axis=-1).astype(BF16),
        "b_rg": row(b_rg), "b_ig": row(b_ig), "lru_lambda": row(lru_lambda),
        "ple_norm": row(ple_norm),
    }


def kernel(x, p, positions, attn_norm, w_in, q_a_norm, w_q_b, kv_a_norm, w_kv_b, conv_w, conv_b,
           w_rg, b_rg, w_ig, b_ig, lru_lambda, w_o_mla, w_o_lru, w_out, ple_norm, w_ple_gate,
           w_ple, final_norm):
    assert x.shape == (1, SEQ, D_MODEL) and p.shape == (DEPTH, 1, SEQ, PLE_DIM)
    cs = _rope_table(positions)
    xs = x.reshape(SEQ, D_MODEL)
    h = _first_norm(xs, attn_norm[0].reshape(1, D_MODEL))
    w_in_t = jnp.swapaxes(w_in, 1, 2)
    out = None
    for l in range(DEPTH):
        wl = _layer_weights(l, q_a_norm, w_q_b, kv_a_norm, w_kv_b, conv_w, conv_b, w_rg,
                            b_rg, w_ig, b_ig, lru_lambda, ple_norm)
        q, k, vt = _qkv(h, w_in_t, wl, cs, l)
        o = _attention(q, k, vt)
        a = _mla_out(h, o, w_in_t, w_o_mla, l)
        b = _lru(h, w_in_t, w_o_lru, wl, l)
        last = l == DEPTH - 1
        gain = (final_norm if last else attn_norm[l + 1]).reshape(1, D_MODEL)
        res = _final(xs, a, b, p, w_out, wl["ple_norm"], w_ple_gate, w_ple, gain, l, last)
        if last:
            out = res[0]
        else:
            xs, h = res
    return out.reshape(1, SEQ, D_MODEL)
```

```python
import functools
import math

import jax
import jax.numpy as jnp
from jax import lax
from jax.experimental import pallas as pl
from jax.experimental.pallas import tpu as pltpu

F32 = jnp.float32
BF16 = jnp.bfloat16

D_MODEL = 2048
SEQ = 8192
DEPTH = 4
CHUNK = 64
N_HEADS = 16
Q_LORA = 512
KV_LORA = 512
QK_NOPE = 128
QK_ROPE = 64
V_HEAD = 128
ROPE_THETA = 10000.0
LRU_BLOCKS = 16
LRU_BLOCK_DIM = 128
CONV_K = 4
LRU_C = 8.0
PLE_DIM = 256
EPS = 1e-6

O_QLAT, O_KVLAT, O_KR, O_GMLA, O_U, O_GLRU, O_MMLA, O_MLRU, O_END = (
    0, 512, 1024, 1088, 3136, 5184, 7232, 9280, 11328)

LANES = 128
SUBLANES = 8
HEAD_PAD = 256
VMEM_LIMIT = 56 * 1024 * 1024

TM = 256
TQ = 1024
TK = 512
HB = 1
NCH = 512
W_ROWS = 128
W_SLOTS = 4


def _resident(shape):
    return pl.BlockSpec(shape, lambda *_: (0,) * len(shape), pipeline_mode=pl.Buffered(1))


def _params(*sem):
    return pltpu.CompilerParams(dimension_semantics=sem, vmem_limit_bytes=VMEM_LIMIT)


def _sigmoid(x):
    return 1.0 / (1.0 + jnp.exp(-x))


def _silu(x):
    return x * _sigmoid(x)


def _rms(x, g):
    return x * lax.rsqrt(jnp.mean(x * x, axis=-1, keepdims=True) + EPS) * g


def _dot(a, b):
    return jnp.dot(a, b, preferred_element_type=F32)


def _hbm():
    return pl.BlockSpec(memory_space=pl.ANY)


def _stage_weight(src, layer, dst, stage, sem):
    rows, width = dst.shape
    assert rows % W_ROWS == 0 and width <= stage.shape[2]

    def copy(ci):
        return pltpu.make_async_copy(
            src.at[layer, pl.ds(ci * W_ROWS, W_ROWS), :],
            stage.at[ci % W_SLOTS, :, pl.ds(0, width)], sem.at[ci % W_SLOTS])

    n = rows // W_ROWS
    for ci in range(min(W_SLOTS - 1, n)):
        copy(ci).start()
    for ci in range(n):
        if ci + W_SLOTS - 1 < n:
            copy(ci + W_SLOTS - 1).start()
        copy(ci).wait()
        dst[ci * W_ROWS:(ci + 1) * W_ROWS, :] = stage[ci % W_SLOTS, :, 0:width].astype(BF16)


def _stage_weight_t(src_t, layer, row0, width, dst, dst_col0, stage, sem):
    assert width % W_ROWS == 0 and row0 % SUBLANES == 0 and dst.shape[0] == stage.shape[2]

    def copy(ci):
        return pltpu.make_async_copy(
            src_t.at[layer, pl.ds(row0 + ci * W_ROWS, W_ROWS), :],
            stage.at[ci % W_SLOTS], sem.at[ci % W_SLOTS])

    n = width // W_ROWS
    for ci in range(min(W_SLOTS - 1, n)):
        copy(ci).start()
    for ci in range(n):
        if ci + W_SLOTS - 1 < n:
            copy(ci + W_SLOTS - 1).start()
        copy(ci).wait()
        cols = slice(dst_col0 + ci * W_ROWS, dst_col0 + (ci + 1) * W_ROWS)
        dst[:, cols] = stage[ci % W_SLOTS].T.astype(BF16)


def _stage_scratch():
    return [pltpu.VMEM((W_SLOTS, W_ROWS, D_MODEL), F32), pltpu.SemaphoreType.DMA((W_SLOTS,))]


def _rope_kernel(pos_ref, invf_ref, cs_ref):
    ang = pos_ref[...].astype(F32) * invf_ref[...]
    lane = lax.broadcasted_iota(jnp.int32, ang.shape, 1)
    c = jnp.cos(ang)
    s = jnp.sin(ang)
    cs_ref[...] = jnp.where(lane < 64, c, jnp.where(lane < 96, -s, s))


def _rope_table(positions):
    rows = 1024
    inv_freq = ROPE_THETA ** (-jnp.arange(0, QK_ROPE, 2, dtype=F32) / QK_ROPE)
    invf = jnp.tile(inv_freq, 4).reshape(1, LANES)
    pos = positions.reshape(SEQ, 1)
    return pl.pallas_call(
        _rope_kernel,
        grid=(SEQ // rows,),
        in_specs=[pl.BlockSpec((rows, 1), lambda i: (i, 0)), _resident((1, LANES))],
        out_specs=pl.BlockSpec((rows, LANES), lambda i: (i, 0)),
        out_shape=jax.ShapeDtypeStruct((SEQ, LANES), F32),
        compiler_params=_params("parallel"),
        name="rope_table",
    )(pos, invf)


def _norm_kernel(x_ref, g_ref, h_ref):
    h_ref[...] = _rms(x_ref[...], g_ref[...]).astype(h_ref.dtype)


def _first_norm(x, g):
    return pl.pallas_call(
        _norm_kernel,
        grid=(SEQ // TM,),
        in_specs=[pl.BlockSpec((TM, D_MODEL), lambda i: (i, 0)), _resident((1, D_MODEL))],
        out_specs=pl.BlockSpec((TM, D_MODEL), lambda i: (i, 0)),
        out_shape=jax.ShapeDtypeStruct((SEQ, D_MODEL), BF16),
        compiler_params=_params("parallel"),
        name="first_norm",
    )(x, g)


def _qkv_kernel(h_ref, win_hbm, qg_ref, kvg_ref, wqn_ref, wqr_ref, wkn_ref, wvt_ref, cs_ref,
                q_ref, k_ref, vt_ref, wlat_ref, stage, sem, *, layer):
    @pl.when(pl.program_id(0) == 0)
    def _():
        _stage_weight_t(win_hbm, layer, O_QLAT, 2 * Q_LORA, wlat_ref, 0, stage, sem)
        half = QK_ROPE // 2
        pieces = ((O_KR, 0, QK_ROPE), (O_KR + half, QK_ROPE, half), (O_KR, QK_ROPE + half, half))
        copies = [pltpu.make_async_copy(win_hbm.at[layer, pl.ds(r0, n), :],
                                        stage.at[0, pl.ds(d0, n), :], sem.at[0])
                  for r0, d0, n in pieces]
        for cp in copies:
            cp.start()
        for cp in copies:
            cp.wait()
        wlat_ref[:, 2 * Q_LORA:2 * Q_LORA + LANES] = stage[0].T.astype(BF16)

    scale = math.log2(math.e) / math.sqrt(QK_NOPE + QK_ROPE)
    lat = _dot(h_ref[...], wlat_ref[...])
    qn = _rms(lat[:, 0:Q_LORA], qg_ref[...]).astype(BF16)
    kvn = _rms(lat[:, Q_LORA:Q_LORA + KV_LORA], kvg_ref[...]).astype(BF16)
    cs = cs_ref[...]
    lane = lax.broadcasted_iota(jnp.int32, cs.shape, 1)
    kr = lat[:, 2 * Q_LORA:2 * Q_LORA + LANES] * cs
    kr = kr + pltpu.roll(kr, 64, 1)
    kr = jnp.where(lane < 64, kr, 0.0).astype(BF16)
    vt_ref[0] = lax.dot_general(wvt_ref[...], kvn, (((1,), (1,)), ((), ())),
                                preferred_element_type=F32).astype(BF16)
    for hp in range(N_HEADS // 2):
        cols = slice(hp * 256, (hp + 1) * 256)
        qnope = _dot(qn, wqn_ref[:, cols]) * scale
        qrope = _dot(qn, wqr_ref[:, cols])
        knope = _dot(kvn, wkn_ref[:, cols])
        for j in range(2):
            hd = 2 * hp + j
            sub = slice(j * LANES, (j + 1) * LANES)
            r = qrope[:, sub] * cs
            r = (r + pltpu.roll(r, 64, 1)) * scale
            q_ref[hd, :, 0:LANES] = qnope[:, sub].astype(BF16)
            q_ref[hd, :, LANES:HEAD_PAD] = r.astype(BF16)
            k_ref[hd, :, 0:LANES] = knope[:, sub].astype(BF16)
            k_ref[hd, :, LANES:HEAD_PAD] = kr


def _qkv(h, w_in, wl, cs, layer):
    return pl.pallas_call(
        functools.partial(_qkv_kernel, layer=layer),
        grid=(SEQ // TK,),
        in_specs=[
            pl.BlockSpec((TK, D_MODEL), lambda i: (i, 0)),
            _hbm(),
            _resident((1, Q_LORA)), _resident((1, KV_LORA)),
            _resident((Q_LORA, D_MODEL)), _resident((Q_LORA, D_MODEL)),
            _resident((KV_LORA, D_MODEL)), _resident((N_HEADS * V_HEAD, KV_LORA)),
            pl.BlockSpec((TK, LANES), lambda i: (i, 0)),
        ],
        out_specs=[
            pl.BlockSpec((N_HEADS, TK, HEAD_PAD), lambda i: (0, i, 0)),
            pl.BlockSpec((N_HEADS, TK, HEAD_PAD), lambda i: (0, i, 0)),
            pl.BlockSpec((1, N_HEADS * V_HEAD, TK), lambda i: (i, 0, 0)),
        ],
        out_shape=[
            jax.ShapeDtypeStruct((N_HEADS, SEQ, HEAD_PAD), BF16),
            jax.ShapeDtypeStruct((N_HEADS, SEQ, HEAD_PAD), BF16),
            jax.ShapeDtypeStruct((SEQ // TK, N_HEADS * V_HEAD, TK), BF16),
        ],
        scratch_shapes=[
            pltpu.VMEM((D_MODEL, 2 * Q_LORA + LANES), BF16),
        ] + _stage_scratch(),
        compiler_params=_params("arbitrary"),
        name="qkv",
    )(h, w_in, wl["q_a_norm"], wl["kv_a_norm"], wl["w_qn"], wl["w_qr"], wl["w_kn"],
      wl["w_vt"], cs)


def _attn_kernel(q_ref, k_ref, vt_ref, o_ref, s_buf, p_buf, alpha_s, mb_s, m_s, l_s, acc_s,
                 bias_s):
    qi = pl.program_id(1)
    n_full = qi * (TQ // TK)
    n_blocks = n_full + TQ // TK

    @pl.when((pl.program_id(0) == 0) & (qi == 0))
    def _():
        for d in range(TQ // TK):
            key = (d * TK + lax.broadcasted_iota(jnp.int32, (TK, TQ), 0)) // CHUNK
            qry = lax.broadcasted_iota(jnp.int32, (TK, TQ), 1) // CHUNK
            bias_s[d] = jnp.where(key <= qry, 0.0, -jnp.inf).astype(F32)

    m_s[...] = jnp.full(m_s.shape, -jnp.inf, F32)
    l_s[...] = jnp.zeros(l_s.shape, F32)
    acc_s[...] = jnp.zeros(acc_s.shape, F32)

    def scores(t, par, diag):
        for hb in range(HB):
            kt = k_ref[hb, pl.ds(pl.multiple_of(t * TK, TK), TK), :]
            st = lax.dot_general(kt, q_ref[hb], (((1,), (1,)), ((), ())),
                                 preferred_element_type=F32)
            if diag is not None:
                st = st + bias_s[diag]
            s_buf[hb, par] = st
            mb_s[hb, par] = jnp.max(st, axis=0, keepdims=True)

    def softmax(par):
        for hb in range(HB):
            m_old = m_s[hb]
            m_new = jnp.maximum(m_old, mb_s[hb, par])
            alpha = jnp.exp2(m_old - m_new)
            alpha_s[hb, par] = alpha
            m_s[hb] = m_new
            m_rows = jnp.broadcast_to(m_new, (SUBLANES, TQ))
            lsum = jnp.zeros((SUBLANES, TQ), F32)
            for r in range(0, TK, 2 * SUBLANES):
                lo = jnp.exp2(s_buf[hb, par, r:r + SUBLANES, :] - m_rows)
                hi = jnp.exp2(s_buf[hb, par, r + SUBLANES:r + 2 * SUBLANES, :] - m_rows)
                lsum = lsum + (lo + hi)
                p_buf[hb, par, r:r + 2 * SUBLANES, :] = (
                    jnp.concatenate([lo, hi], axis=0).astype(BF16))
            l_s[hb] = alpha * l_s[hb] + jnp.sum(lsum, axis=0, keepdims=True)

    def values(t, par):
        for hb in range(HB):
            vt = vt_ref[t, hb * V_HEAD:(hb + 1) * V_HEAD, :]
            for c in range(0, TQ, NCH):
                acc_s[hb, :, c:c + NCH] = (alpha_s[hb, par, :, c:c + NCH] * acc_s[hb, :, c:c + NCH]
                                           + _dot(vt, p_buf[hb, par, :, c:c + NCH]))

    def tick(t, par, diag):
        scores(t, par, diag)
        values(t - 2, par)
        softmax(1 - par)

    @pl.when(qi == 0)
    def _():
        scores(0, 0, 0)
        softmax(0)
        scores(1, 1, 1)

    @pl.when(qi > 0)
    def _():
        scores(0, 0, None)
        softmax(0)
        scores(1, 1, None)

        def pair(i, carry):
            t = 2 + 2 * i
            tick(t, 0, None)
            tick(t + 1, 1, None)
            return carry

        lax.fori_loop(0, (n_full - 2) // 2, pair, 0)
        tick(n_full, 0, 0)
        tick(n_full + 1, 1, 1)

    values(n_blocks - 2, 0)
    softmax(1)
    values(n_blocks - 1, 1)
    for hb in range(HB):
        o_ref[:, hb * V_HEAD:(hb + 1) * V_HEAD] = (acc_s[hb] / l_s[hb]).T.astype(o_ref.dtype)


def _attention(q, k, vt):
    return pl.pallas_call(
        _attn_kernel,
        grid=(N_HEADS // HB, SEQ // TQ),
        in_specs=[
            pl.BlockSpec((HB, TQ, HEAD_PAD), lambda h, i: (h, i, 0)),
            pl.BlockSpec((HB, SEQ, HEAD_PAD), lambda h, i: (h, 0, 0)),
            pl.BlockSpec((SEQ // TK, HB * V_HEAD, TK), lambda h, i: (0, h, 0)),
        ],
        out_specs=pl.BlockSpec((TQ, HB * V_HEAD), lambda h, i: (i, h)),
        out_shape=jax.ShapeDtypeStruct((SEQ, N_HEADS * V_HEAD), F32),
        scratch_shapes=[
            pltpu.VMEM((HB, 2, TK, TQ), F32),
            pltpu.VMEM((HB, 2, TK, TQ), BF16),
            pltpu.VMEM((HB, 2, 1, TQ), F32),
            pltpu.VMEM((HB, 2, 1, TQ), F32),
            pltpu.VMEM((HB, 1, TQ), F32),
            pltpu.VMEM((HB, 1, TQ), F32),
            pltpu.VMEM((HB, V_HEAD, TQ), F32),
            pltpu.VMEM((TQ // TK, TK, TQ), F32),
        ],
        compiler_params=_params("arbitrary", "arbitrary"),
        name="attn",
    )(q, k, vt)


def _mla_out_kernel(h_ref, o_ref, win_hbm, wo_hbm, a_ref, wgm_ref, wo_ref, og_ref, stage, sem,
                    *, layer):
    @pl.when(pl.program_id(0) == 0)
    def _():
        _stage_weight_t(win_hbm, layer, O_GMLA, D_MODEL, wgm_ref, 0, stage, sem)
        _stage_weight_t(win_hbm, layer, O_MMLA, D_MODEL, wgm_ref, D_MODEL, stage, sem)
        _stage_weight(wo_hbm, layer, wo_ref, stage, sem)

    h = h_ref[...]
    for c in range(D_MODEL // NCH):
        cols = slice(c * NCH, (c + 1) * NCH)
        g = _dot(h, wgm_ref[:, cols])
        og_ref[:, cols] = (o_ref[:, cols] * _silu(g)).astype(BF16)
    og = og_ref[...]
    for c in range(D_MODEL // NCH):
        cols = slice(c * NCH, (c + 1) * NCH)
        mm = _dot(h, wgm_ref[:, D_MODEL + c * NCH:D_MODEL + (c + 1) * NCH])
        a_ref[:, cols] = _sigmoid(mm) * _dot(og, wo_ref[:, cols])


def _mla_out(h, o, w_in, w_o_mla, layer):
    return pl.pallas_call(
        functools.partial(_mla_out_kernel, layer=layer),
        grid=(SEQ // TM,),
        in_specs=[
            pl.BlockSpec((TM, D_MODEL), lambda i: (i, 0)),
            pl.BlockSpec((TM, D_MODEL), lambda i: (i, 0)),
            _hbm(), _hbm(),
        ],
        out_specs=pl.BlockSpec((TM, D_MODEL), lambda i: (i, 0)),
        out_shape=jax.ShapeDtypeStruct((SEQ, D_MODEL), F32),
        scratch_shapes=[
            pltpu.VMEM((D_MODEL, 2 * D_MODEL), BF16),
            pltpu.VMEM((D_MODEL, D_MODEL), BF16),
            pltpu.VMEM((TM, D_MODEL), BF16),
        ] + _stage_scratch(),
        compiler_params=_params("arbitrary"),
        name="mla_out",
    )(h, o, w_in, w_o_mla)


def _lru_kernel(h_ref, win_hbm, cw_ref, cb_ref, wg_ref, brg_ref, big_ref, lam_ref, wo_hbm,
                b_out_ref, win_ref, wo_ref, ubuf, a_s, b_s, hg_s, ml_s, hcar, stage, sem, *, layer):
    i = pl.program_id(0)

    @pl.when(i == 0)
    def _():
        _stage_weight_t(win_hbm, layer, O_U, D_MODEL, win_ref, 0, stage, sem)
        _stage_weight_t(win_hbm, layer, O_GLRU, D_MODEL, win_ref, D_MODEL, stage, sem)
        _stage_weight_t(win_hbm, layer, O_MLRU, D_MODEL, win_ref, 2 * D_MODEL, stage, sem)
        _stage_weight(wo_hbm, layer, wo_ref, stage, sem)
        ubuf[0:SUBLANES, :] = jnp.zeros((SUBLANES, D_MODEL), F32)
        hcar[...] = jnp.zeros_like(hcar)

    h = h_ref[...]
    for c in range(D_MODEL // NCH):
        cols = slice(c * NCH, (c + 1) * NCH)
        ubuf[SUBLANES:SUBLANES + TM, cols] = _dot(h, win_ref[:, cols])

    nl = -lam_ref[...]
    softplus = jnp.maximum(nl, 0.0) + jnp.log1p(jnp.exp(-jnp.abs(nl)))

    g_per = LRU_BLOCKS // (D_MODEL // NCH)
    for j in range(LRU_BLOCKS):
        cols = slice(j * LRU_BLOCK_DIM, (j + 1) * LRU_BLOCK_DIM)
        xc = cb_ref[:, cols] + cw_ref[CONV_K - 1:CONV_K, cols] * ubuf[SUBLANES:SUBLANES + TM, cols]
        for kk in range(CONV_K - 1):
            off = SUBLANES - (CONV_K - 1) + kk
            xc = xc + cw_ref[kk:kk + 1, cols] * ubuf[off:off + TM, cols]
        ri = _dot(xc.astype(BF16), wg_ref[j])
        r = _sigmoid(ri[:, 0:LRU_BLOCK_DIM] + brg_ref[:, cols])
        ig = _sigmoid(ri[:, LRU_BLOCK_DIM:] + big_ref[:, cols])
        log_a = (-LRU_C * r) * softplus[:, cols]
        a = jnp.exp(log_a)
        a_s[:, cols] = a
        b_s[:, cols] = jnp.sqrt(-jnp.tanh(log_a) * (a * a + 1.0)) * (ig * xc)
        if j % g_per == g_per - 1:
            c = j // g_per
            g = _dot(h, win_ref[:, D_MODEL + c * NCH:D_MODEL + (c + 1) * NCH])
            hg_s[:, c * NCH:(c + 1) * NCH] = _silu(g).astype(BF16)

    ubuf[0:SUBLANES, :] = ubuf[TM:TM + SUBLANES, :]

    SW = 512
    n_rg = TM // SUBLANES
    row = lax.broadcasted_iota(jnp.int32, (SUBLANES, SW), 0)
    keep = {d: row >= d for d in (1, 2, 4)}
    hprev = [hcar[:, c * SW:(c + 1) * SW] for c in range(D_MODEL // SW)]
    ml_every = n_rg // (D_MODEL // NCH)
    for rg in range(n_rg):
        rows = slice(rg * SUBLANES, (rg + 1) * SUBLANES)
        for c in range(D_MODEL // SW):
            cols = slice(c * SW, (c + 1) * SW)
            a8 = a_s[rows, cols]
            b8 = b_s[rows, cols]
            for d in (1, 2, 4):
                b8 = jnp.where(keep[d], a8 * pltpu.roll(b8, d, 0) + b8, b8)
                a8 = jnp.where(keep[d], a8 * pltpu.roll(a8, d, 0), a8)
            hs = a8 * hprev[c] + b8
            b_s[rows, cols] = hs
            hprev[c] = jnp.broadcast_to(hs[SUBLANES - 1:SUBLANES, :], (SUBLANES, SW))
        if rg % ml_every == ml_every - 1:
            c = rg // ml_every
            ml = _dot(h, win_ref[:, 2 * D_MODEL + c * NCH:2 * D_MODEL + (c + 1) * NCH])
            ml_s[:, c * NCH:(c + 1) * NCH] = _sigmoid(ml)
    for c in range(D_MODEL // SW):
        hcar[:, c * SW:(c + 1) * SW] = hprev[c]

    for c in range(D_MODEL // NCH):
        cols = slice(c * NCH, (c + 1) * NCH)
        hg_s[:, cols] = (b_s[:, cols] * hg_s[:, cols].astype(F32)).astype(BF16)
    hg = hg_s[...]
    for c in range(D_MODEL // NCH):
        cols = slice(c * NCH, (c + 1) * NCH)
        b_out_ref[:, cols] = ml_s[:, cols] * _dot(hg, wo_ref[:, cols])


def _lru(h, w_in, w_o_lru, wl, layer):
    return pl.pallas_call(
        functools.partial(_lru_kernel, layer=layer),
        grid=(SEQ // TM,),
        in_specs=[
            pl.BlockSpec((TM, D_MODEL), lambda i: (i, 0)),
            _hbm(),
            _resident((CONV_K, D_MODEL)), _resident((1, D_MODEL)),
            _resident((LRU_BLOCKS, LRU_BLOCK_DIM, 2 * LRU_BLOCK_DIM)),
            _resident((1, D_MODEL)), _resident((1, D_MODEL)), _resident((1, D_MODEL)),
            _hbm(),
        ],
        out_specs=pl.BlockSpec((TM, D_MODEL), lambda i: (i, 0)),
        out_shape=jax.ShapeDtypeStruct((SEQ, D_MODEL), F32),
        scratch_shapes=[
            pltpu.VMEM((D_MODEL, 3 * D_MODEL), BF16),
            pltpu.VMEM((D_MODEL, D_MODEL), BF16),
            pltpu.VMEM((TM + SUBLANES, D_MODEL), F32),
            pltpu.VMEM((TM, D_MODEL), F32),
            pltpu.VMEM((TM, D_MODEL), F32),
            pltpu.VMEM((TM, D_MODEL), BF16),
            pltpu.VMEM((TM, D_MODEL), F32),
            pltpu.VMEM((SUBLANES, D_MODEL), F32),
        ] + _stage_scratch(),
        compiler_params=_params("arbitrary"),
        name="lru",
    )(h, w_in, wl["conv_w"], wl["conv_b"], wl["w_gates"], wl["b_rg"], wl["b_ig"],
      wl["lru_lambda"], w_o_lru)


def _final_kernel(x_ref, a_ref, b_ref, p_ref, wout_hbm, pg_ref, wpg_hbm, wple_hbm, ng_ref,
                  *refs, layer, last):
    n_out = 1 if last else 2
    out_refs = refs[:n_out]
    wout_ref, wpg_ref, wple_ref, stage, sem = refs[n_out:]

    @pl.when(pl.program_id(0) == 0)
    def _():
        _stage_weight(wout_hbm, layer, wout_ref, stage, sem)
        _stage_weight(wpg_hbm, layer, wpg_ref, stage, sem)
        _stage_weight(wple_hbm, layer, wple_ref, stage, sem)

    merged = (a_ref[...] + b_ref[...]).astype(BF16)
    x1 = x_ref[...] + _dot(merged, wout_ref[...])
    n1 = _rms(x1, pg_ref[...]).astype(BF16)
    gate = _sigmoid(_dot(n1, wpg_ref[...]))
    x2 = x1 + _dot(p_ref[...].astype(BF16), wple_ref[...]) * gate
    normed = _rms(x2, ng_ref[...])
    if last:
        out_refs[0][...] = normed
    else:
        out_refs[0][...] = x2
        out_refs[1][...] = normed.astype(BF16)


def _final(x, a, b, p, w_out, ple_gain, w_ple_gate, w_ple, next_gain, layer, last):
    row = pl.BlockSpec((TM, D_MODEL), lambda i: (i, 0))
    if last:
        out_specs = [row]
        out_shape = [jax.ShapeDtypeStruct((SEQ, D_MODEL), F32)]
    else:
        out_specs = [row, row]
        out_shape = [jax.ShapeDtypeStruct((SEQ, D_MODEL), F32),
                     jax.ShapeDtypeStruct((SEQ, D_MODEL), BF16)]
    return pl.pallas_call(
        functools.partial(_final_kernel, layer=layer, last=last),
        grid=(SEQ // TM,),
        in_specs=[
            row, row, row,
            pl.BlockSpec((None, None, TM, PLE_DIM), lambda i: (layer, 0, i, 0)),
            _hbm(), _resident((1, D_MODEL)), _hbm(), _hbm(), _resident((1, D_MODEL)),
        ],
        out_specs=out_specs,
        out_shape=out_shape,
        scratch_shapes=[
            pltpu.VMEM((D_MODEL, D_MODEL), BF16),
            pltpu.VMEM((D_MODEL, D_MODEL), BF16),
            pltpu.VMEM((PLE_DIM, D_MODEL), BF16),
        ] + _stage_scratch(),
        compiler_params=_params("arbitrary"),
        name="final",
    )(x, a, b, p, w_out, ple_gain, w_ple_gate, w_ple, next_gain)


def _swap_halves(w):
    half = QK_ROPE // 2
    return jnp.concatenate([w[..., half:], w[..., :half]], axis=-1)


def _layer_weights(l, q_a_norm, w_q_b, kv_a_norm, w_kv_b, conv_w, conv_b, w_rg, b_rg,
                   w_ig, b_ig, lru_lambda, ple_norm):
    wq = w_q_b[l].reshape(Q_LORA, N_HEADS, QK_NOPE + QK_ROPE)
    wq_r = wq[:, :, QK_NOPE:]
    wkv = w_kv_b[l].reshape(KV_LORA, N_HEADS, QK_NOPE + V_HEAD)
    row = lambda v: v[l].reshape(1, -1)
    return {
        "q_a_norm": row(q_a_norm), "kv_a_norm": row(kv_a_norm),
        "w_qn": wq[:, :, :QK_NOPE].reshape(Q_LORA, D_MODEL).astype(BF16),
        "w_qr": jnp.concatenate([wq_r, _swap_halves(wq_r)], axis=-1).reshape(Q_LORA, D_MODEL).astype(BF16),
        "w_kn": wkv[:, :, :QK_NOPE].reshape(KV_LORA, D_MODEL).astype(BF16),
        "w_vt": wkv[:, :, QK_NOPE:].reshape(KV_LORA, D_MODEL).T.astype(BF16),
        "conv_w": conv_w[l], "conv_b": row(conv_b),
        "w_gates": jnp.concatenate([w_rg[l], w_ig[l]], axis=-1).astype(BF16),
        "b_rg": row(b_rg), "b_ig": row(b_ig), "lru_lambda": row(lru_lambda),
        "ple_norm": row(ple_norm),
    }


def kernel(x, p, positions, attn_norm, w_in, q_a_norm, w_q_b, kv_a_norm, w_kv_b, conv_w, conv_b,
           w_rg, b_rg, w_ig, b_ig, lru_lambda, w_o_mla, w_o_lru, w_out, ple_norm, w_ple_gate,
           w_ple, final_norm):
    assert x.shape == (1, SEQ, D_MODEL) and p.shape == (DEPTH, 1, SEQ, PLE_DIM)
    cs = _rope_table(positions)
    xs = x.reshape(SEQ, D_MODEL)
    h = _first_norm(xs, attn_norm[0].reshape(1, D_MODEL))
    w_in_t = jnp.swapaxes(w_in, 1, 2)
    out = None
    for l in range(DEPTH):
        wl = _layer_weights(l, q_a_norm, w_q_b, kv_a_norm, w_kv_b, conv_w, conv_b, w_rg,
                            b_rg, w_ig, b_ig, lru_lambda, ple_norm)
        q, k, vt = _qkv(h, w_in_t, wl, cs, l)
        o = _attention(q, k, vt)
        a = _mla_out(h, o, w_in_t, w_o_mla, l)
        b = _lru(h, w_in_t, w_o_lru, wl, l)
        last = l == DEPTH - 1
        gain = (final_norm if last else attn_norm[l + 1]).reshape(1, D_MODEL)
        res = _final(xs, a, b, p, w_out, wl["ple_norm"], w_ple_gate, w_ple, gain, l, last)
        if last:
            out = res[0]
        else:
            xs, h = res
    return out.reshape(1, SEQ, D_MODEL)
```

```python
import functools
import math

import jax
import jax.numpy as jnp
from jax import lax
from jax.experimental import pallas as pl
from jax.experimental.pallas import tpu as pltpu

F32 = jnp.float32
BF16 = jnp.bfloat16

D_MODEL = 2048
SEQ = 8192
DEPTH = 4
CHUNK = 64
N_HEADS = 16
Q_LORA = 512
KV_LORA = 512
QK_NOPE = 128
QK_ROPE = 64
V_HEAD = 128
ROPE_THETA = 10000.0
LRU_BLOCKS = 16
LRU_BLOCK_DIM = 128
CONV_K = 4
LRU_C = 8.0
PLE_DIM = 256
EPS = 1e-6

O_QLAT, O_KVLAT, O_KR, O_GMLA, O_U, O_GLRU, O_MMLA, O_MLRU, O_END = (
    0, 512, 1024, 1088, 3136, 5184, 7232, 9280, 11328)

LANES = 128
SUBLANES = 8
HEAD_PAD = 256
VMEM_LIMIT = 56 * 1024 * 1024

TM = 256
TQ = 1024
TK = 512
HB = 1
V_ROWS = V_HEAD + 16
NCH = 512
W_ROWS = 128
W_SLOTS = 4


def _resident(shape):
    return pl.BlockSpec(shape, lambda *_: (0,) * len(shape), pipeline_mode=pl.Buffered(1))


def _params(*sem):
    return pltpu.CompilerParams(dimension_semantics=sem, vmem_limit_bytes=VMEM_LIMIT)


def _sigmoid(x):
    return 1.0 / (1.0 + jnp.exp(-x))


def _silu(x):
    return x * _sigmoid(x)


def _rms(x, g):
    return x * lax.rsqrt(jnp.mean(x * x, axis=-1, keepdims=True) + EPS) * g


def _dot(a, b):
    return jnp.dot(a, b, preferred_element_type=F32)


def _hbm():
    return pl.BlockSpec(memory_space=pl.ANY)


def _stage_weight(src, layer, dst, stage, sem):
    rows, width = dst.shape
    assert rows % W_ROWS == 0 and width <= stage.shape[2]

    def copy(ci):
        return pltpu.make_async_copy(
            src.at[layer, pl.ds(ci * W_ROWS, W_ROWS), :],
            stage.at[ci % W_SLOTS, :, pl.ds(0, width)], sem.at[ci % W_SLOTS])

    n = rows // W_ROWS
    for ci in range(min(W_SLOTS - 1, n)):
        copy(ci).start()
    for ci in range(n):
        if ci + W_SLOTS - 1 < n:
            copy(ci + W_SLOTS - 1).start()
        copy(ci).wait()
        dst[ci * W_ROWS:(ci + 1) * W_ROWS, :] = stage[ci % W_SLOTS, :, 0:width].astype(BF16)


def _stage_weight_t(src_t, layer, row0, width, dst, dst_col0, stage, sem):
    assert width % W_ROWS == 0 and row0 % SUBLANES == 0 and dst.shape[0] == stage.shape[2]

    def copy(ci):
        return pltpu.make_async_copy(
            src_t.at[layer, pl.ds(row0 + ci * W_ROWS, W_ROWS), :],
            stage.at[ci % W_SLOTS], sem.at[ci % W_SLOTS])

    n = width // W_ROWS
    for ci in range(min(W_SLOTS - 1, n)):
        copy(ci).start()
    for ci in range(n):
        if ci + W_SLOTS - 1 < n:
            copy(ci + W_SLOTS - 1).start()
        copy(ci).wait()
        cols = slice(dst_col0 + ci * W_ROWS, dst_col0 + (ci + 1) * W_ROWS)
        dst[:, cols] = stage[ci % W_SLOTS].T.astype(BF16)


def _stage_scratch():
    return [pltpu.VMEM((W_SLOTS, W_ROWS, D_MODEL), F32), pltpu.SemaphoreType.DMA((W_SLOTS,))]


def _rope_kernel(pos_ref, invf_ref, cs_ref):
    ang = pos_ref[...].astype(F32) * invf_ref[...]
    lane = lax.broadcasted_iota(jnp.int32, ang.shape, 1)
    c = jnp.cos(ang)
    s = jnp.sin(ang)
    cs_ref[...] = jnp.where(lane < 64, c, jnp.where(lane < 96, -s, s))


def _rope_table(positions):
    rows = 1024
    inv_freq = ROPE_THETA ** (-jnp.arange(0, QK_ROPE, 2, dtype=F32) / QK_ROPE)
    invf = jnp.tile(inv_freq, 4).reshape(1, LANES)
    pos = positions.reshape(SEQ, 1)
    return pl.pallas_call(
        _rope_kernel,
        grid=(SEQ // rows,),
        in_specs=[pl.BlockSpec((rows, 1), lambda i: (i, 0)), _resident((1, LANES))],
        out_specs=pl.BlockSpec((rows, LANES), lambda i: (i, 0)),
        out_shape=jax.ShapeDtypeStruct((SEQ, LANES), F32),
        compiler_params=_params("parallel"),
        name="rope_table",
    )(pos, invf)


def _norm_kernel(x_ref, g_ref, h_ref):
    h_ref[...] = _rms(x_ref[...], g_ref[...]).astype(h_ref.dtype)


def _first_norm(x, g):
    return pl.pallas_call(
        _norm_kernel,
        grid=(SEQ // TM,),
        in_specs=[pl.BlockSpec((TM, D_MODEL), lambda i: (i, 0)), _resident((1, D_MODEL))],
        out_specs=pl.BlockSpec((TM, D_MODEL), lambda i: (i, 0)),
        out_shape=jax.ShapeDtypeStruct((SEQ, D_MODEL), BF16),
        compiler_params=_params("parallel"),
        name="first_norm",
    )(x, g)


def _qkv_kernel(h_ref, win_hbm, qg_ref, kvg_ref, wqn_ref, wqr_ref, wkn_ref, wvt_ref, cs_ref,
                q_ref, k_ref, vt_ref, wlat_ref, stage, sem, *, layer):
    @pl.when(pl.program_id(0) == 0)
    def _():
        _stage_weight_t(win_hbm, layer, O_QLAT, 2 * Q_LORA, wlat_ref, 0, stage, sem)
        half = QK_ROPE // 2
        pieces = ((O_KR, 0, QK_ROPE), (O_KR + half, QK_ROPE, half), (O_KR, QK_ROPE + half, half))
        copies = [pltpu.make_async_copy(win_hbm.at[layer, pl.ds(r0, n), :],
                                        stage.at[0, pl.ds(d0, n), :], sem.at[0])
                  for r0, d0, n in pieces]
        for cp in copies:
            cp.start()
        for cp in copies:
            cp.wait()
        wlat_ref[:, 2 * Q_LORA:2 * Q_LORA + LANES] = stage[0].T.astype(BF16)

    scale = math.log2(math.e) / math.sqrt(QK_NOPE + QK_ROPE)
    lat = _dot(h_ref[...], wlat_ref[...])
    qn = _rms(lat[:, 0:Q_LORA], qg_ref[...]).astype(BF16)
    kvn = _rms(lat[:, Q_LORA:Q_LORA + KV_LORA], kvg_ref[...]).astype(BF16)
    cs = cs_ref[...]
    lane = lax.broadcasted_iota(jnp.int32, cs.shape, 1)
    kr = lat[:, 2 * Q_LORA:2 * Q_LORA + LANES] * cs
    kr = kr + pltpu.roll(kr, 64, 1)
    kr = jnp.where(lane < 64, kr, 0.0).astype(BF16)
    vt = lax.dot_general(wvt_ref[...], kvn, (((1,), (1,)), ((), ())),
                         preferred_element_type=F32).astype(BF16)
    for hd in range(N_HEADS):
        vt_ref[0, hd * V_ROWS:hd * V_ROWS + V_HEAD, :] = vt[hd * V_HEAD:(hd + 1) * V_HEAD, :]
        vt_ref[0, hd * V_ROWS + V_HEAD:(hd + 1) * V_ROWS, :] = jnp.ones((V_ROWS - V_HEAD, TK), BF16)
    for hp in range(N_HEADS // 2):
        cols = slice(hp * 256, (hp + 1) * 256)
        qnope = _dot(qn, wqn_ref[:, cols]) * scale
        qrope = _dot(qn, wqr_ref[:, cols])
        knope = _dot(kvn, wkn_ref[:, cols])
        for j in range(2):
            hd = 2 * hp + j
            sub = slice(j * LANES, (j + 1) * LANES)
            r = qrope[:, sub] * cs
            r = (r + pltpu.roll(r, 64, 1)) * scale
            q_ref[hd, :, 0:LANES] = qnope[:, sub].astype(BF16)
            q_ref[hd, :, LANES:HEAD_PAD] = r.astype(BF16)
            k_ref[hd, :, 0:LANES] = knope[:, sub].astype(BF16)
            k_ref[hd, :, LANES:HEAD_PAD] = kr


def _qkv(h, w_in, wl, cs, layer):
    return pl.pallas_call(
        functools.partial(_qkv_kernel, layer=layer),
        grid=(SEQ // TK,),
        in_specs=[
            pl.BlockSpec((TK, D_MODEL), lambda i: (i, 0)),
            _hbm(),
            _resident((1, Q_LORA)), _resident((1, KV_LORA)),
            _resident((Q_LORA, D_MODEL)), _resident((Q_LORA, D_MODEL)),
            _resident((KV_LORA, D_MODEL)), _resident((N_HEADS * V_HEAD, KV_LORA)),
            pl.BlockSpec((TK, LANES), lambda i: (i, 0)),
        ],
        out_specs=[
            pl.BlockSpec((N_HEADS, TK, HEAD_PAD), lambda i: (0, i, 0)),
            pl.BlockSpec((N_HEADS, TK, HEAD_PAD), lambda i: (0, i, 0)),
            pl.BlockSpec((1, N_HEADS * V_ROWS, TK), lambda i: (i, 0, 0)),
        ],
        out_shape=[
            jax.ShapeDtypeStruct((N_HEADS, SEQ, HEAD_PAD), BF16),
            jax.ShapeDtypeStruct((N_HEADS, SEQ, HEAD_PAD), BF16),
            jax.ShapeDtypeStruct((SEQ // TK, N_HEADS * V_ROWS, TK), BF16),
        ],
        scratch_shapes=[
            pltpu.VMEM((D_MODEL, 2 * Q_LORA + LANES), BF16),
        ] + _stage_scratch(),
        compiler_params=_params("arbitrary"),
        name="qkv",
    )(h, w_in, wl["q_a_norm"], wl["kv_a_norm"], wl["w_qn"], wl["w_qr"], wl["w_kn"],
      wl["w_vt"], cs)


def _attn_kernel(q_ref, k_ref, vt_ref, o_ref, s_buf, p_buf, alpha_s, mb_s, m_s, acc_s, bias_s):
    qi = pl.program_id(1)
    n_full = qi * (TQ // TK)
    n_blocks = n_full + TQ // TK

    @pl.when((pl.program_id(0) == 0) & (qi == 0))
    def _():
        for d in range(TQ // TK):
            key = (d * TK + lax.broadcasted_iota(jnp.int32, (TK, TQ), 0)) // CHUNK
            qry = lax.broadcasted_iota(jnp.int32, (TK, TQ), 1) // CHUNK
            bias_s[d] = jnp.where(key <= qry, 0.0, -jnp.inf).astype(F32)

    m_s[...] = jnp.full(m_s.shape, -jnp.inf, F32)
    acc_s[...] = jnp.zeros(acc_s.shape, F32)

    def scores(t, par, diag):
        for hb in range(HB):
            kt = k_ref[hb, pl.ds(pl.multiple_of(t * TK, TK), TK), :]
            st = lax.dot_general(kt, q_ref[hb], (((1,), (1,)), ((), ())),
                                 preferred_element_type=F32)
            if diag is not None:
                st = st + bias_s[diag]
            s_buf[hb, par] = st
            mb_s[hb, par] = jnp.max(st, axis=0, keepdims=True)

    def softmax(par):
        for hb in range(HB):
            m_old = m_s[hb]
            m_new = jnp.maximum(m_old, mb_s[hb, par])
            alpha = jnp.exp2(m_old - m_new)
            alpha_s[hb, par] = alpha
            m_s[hb] = m_new
            m_rows = jnp.broadcast_to(m_new, (SUBLANES, TQ))
            for r in range(0, TK, 2 * SUBLANES):
                lo = jnp.exp2(s_buf[hb, par, r:r + SUBLANES, :] - m_rows)
                hi = jnp.exp2(s_buf[hb, par, r + SUBLANES:r + 2 * SUBLANES, :] - m_rows)
                p_buf[hb, par, r:r + 2 * SUBLANES, :] = (
                    jnp.concatenate([lo, hi], axis=0).astype(BF16))

    def values(t, par):
        for hb in range(HB):
            vt = vt_ref[t, hb * V_ROWS:(hb + 1) * V_ROWS, :]
            for c in range(0, TQ, NCH):
                acc_s[hb, :, c:c + NCH] = (alpha_s[hb, par, :, c:c + NCH] * acc_s[hb, :, c:c + NCH]
                                           + _dot(vt, p_buf[hb, par, :, c:c + NCH]))

    def tick(t, par, diag):
        scores(t, par, diag)
        values(t - 2, par)
        softmax(1 - par)

    @pl.when(qi == 0)
    def _():
        scores(0, 0, 0)
        softmax(0)
        scores(1, 1, 1)

    @pl.when(qi > 0)
    def _():
        scores(0, 0, None)
        softmax(0)
        scores(1, 1, None)

        def pair(i, carry):
            t = 2 + 2 * i
            tick(t, 0, None)
            tick(t + 1, 1, None)
            return carry

        lax.fori_loop(0, (n_full - 2) // 2, pair, 0)
        tick(n_full, 0, 0)
        tick(n_full + 1, 1, 1)

    values(n_blocks - 2, 0)
    softmax(1)
    values(n_blocks - 1, 1)
    for hb in range(HB):
        out = acc_s[hb, 0:V_HEAD, :] / acc_s[hb, V_HEAD:V_HEAD + 1, :]
        o_ref[:, hb * V_HEAD:(hb + 1) * V_HEAD] = out.T.astype(o_ref.dtype)


def _attention(q, k, vt):
    return pl.pallas_call(
        _attn_kernel,
        grid=(N_HEADS // HB, SEQ // TQ),
        in_specs=[
            pl.BlockSpec((HB, TQ, HEAD_PAD), lambda h, i: (h, i, 0)),
            pl.BlockSpec((HB, SEQ, HEAD_PAD), lambda h, i: (h, 0, 0)),
            pl.BlockSpec((SEQ // TK, HB * V_ROWS, TK), lambda h, i: (0, h, 0)),
        ],
        out_specs=pl.BlockSpec((TQ, HB * V_HEAD), lambda h, i: (i, h)),
        out_shape=jax.ShapeDtypeStruct((SEQ, N_HEADS * V_HEAD), F32),
        scratch_shapes=[
            pltpu.VMEM((HB, 2, TK, TQ), F32),
            pltpu.VMEM((HB, 2, TK, TQ), BF16),
            pltpu.VMEM((HB, 2, 1, TQ), F32),
            pltpu.VMEM((HB, 2, 1, TQ), F32),
            pltpu.VMEM((HB, 1, TQ), F32),
            pltpu.VMEM((HB, V_ROWS, TQ), F32),
            pltpu.VMEM((TQ // TK, TK, TQ), F32),
        ],
        compiler_params=_params("arbitrary", "arbitrary"),
        name="attn",
    )(q, k, vt)


def _mla_out_kernel(h_ref, o_ref, win_hbm, wo_hbm, a_ref, wgm_ref, wo_ref, og_ref, stage, sem,
                    *, layer):
    @pl.when(pl.program_id(0) == 0)
    def _():
        _stage_weight_t(win_hbm, layer, O_GMLA, D_MODEL, wgm_ref, 0, stage, sem)
        _stage_weight_t(win_hbm, layer, O_MMLA, D_MODEL, wgm_ref, D_MODEL, stage, sem)
        _stage_weight(wo_hbm, layer, wo_ref, stage, sem)

    h = h_ref[...]
    for c in range(D_MODEL // NCH):
        cols = slice(c * NCH, (c + 1) * NCH)
        g = _dot(h, wgm_ref[:, cols])
        og_ref[:, cols] = (o_ref[:, cols] * _silu(g)).astype(BF16)
    og = og_ref[...]
    for c in range(D_MODEL // NCH):
        cols = slice(c * NCH, (c + 1) * NCH)
        mm = _dot(h, wgm_ref[:, D_MODEL + c * NCH:D_MODEL + (c + 1) * NCH])
        a_ref[:, cols] = _sigmoid(mm) * _dot(og, wo_ref[:, cols])


def _mla_out(h, o, w_in, w_o_mla, layer):
    return pl.pallas_call(
        functools.partial(_mla_out_kernel, layer=layer),
        grid=(SEQ // TM,),
        in_specs=[
            pl.BlockSpec((TM, D_MODEL), lambda i: (i, 0)),
            pl.BlockSpec((TM, D_MODEL), lambda i: (i, 0)),
            _hbm(), _hbm(),
        ],
        out_specs=pl.BlockSpec((TM, D_MODEL), lambda i: (i, 0)),
        out_shape=jax.ShapeDtypeStruct((SEQ, D_MODEL), F32),
        scratch_shapes=[
            pltpu.VMEM((D_MODEL, 2 * D_MODEL), BF16),
            pltpu.VMEM((D_MODEL, D_MODEL), BF16),
            pltpu.VMEM((TM, D_MODEL), BF16),
        ] + _stage_scratch(),
        compiler_params=_params("arbitrary"),
        name="mla_out",
    )(h, o, w_in, w_o_mla)


def _lru_kernel(h_ref, win_hbm, cw_ref, cb_ref, wg_ref, brg_ref, big_ref, lam_ref, wo_hbm,
                b_out_ref, win_ref, wo_ref, ubuf, a_s, b_s, hg_s, ml_s, hcar, stage, sem, *, layer):
    i = pl.program_id(0)

    @pl.when(i == 0)
    def _():
        _stage_weight_t(win_hbm, layer, O_U, D_MODEL, win_ref, 0, stage, sem)
        _stage_weight_t(win_hbm, layer, O_GLRU, D_MODEL, win_ref, D_MODEL, stage, sem)
        _stage_weight_t(win_hbm, layer, O_MLRU, D_MODEL, win_ref, 2 * D_MODEL, stage, sem)
        _stage_weight(wo_hbm, layer, wo_ref, stage, sem)
        ubuf[0:SUBLANES, :] = jnp.zeros((SUBLANES, D_MODEL), F32)
        hcar[...] = jnp.zeros_like(hcar)

    h = h_ref[...]
    for c in range(D_MODEL // NCH):
        cols = slice(c * NCH, (c + 1) * NCH)
        ubuf[SUBLANES:SUBLANES + TM, cols] = _dot(h, win_ref[:, cols])

    nl = -lam_ref[...]
    softplus = jnp.maximum(nl, 0.0) + jnp.log1p(jnp.exp(-jnp.abs(nl)))

    g_per = LRU_BLOCKS // (D_MODEL // NCH)
    for j in range(LRU_BLOCKS):
        cols = slice(j * LRU_BLOCK_DIM, (j + 1) * LRU_BLOCK_DIM)
        xc = cb_ref[:, cols] + cw_ref[CONV_K - 1:CONV_K, cols] * ubuf[SUBLANES:SUBLANES + TM, cols]
        for kk in range(CONV_K - 1):
            off = SUBLANES - (CONV_K - 1) + kk
            xc = xc + cw_ref[kk:kk + 1, cols] * ubuf[off:off + TM, cols]
        ri = _dot(xc.astype(BF16), wg_ref[j])
        r = _sigmoid(ri[:, 0:LRU_BLOCK_DIM] + brg_ref[:, cols])
        ig = _sigmoid(ri[:, LRU_BLOCK_DIM:] + big_ref[:, cols])
        log_a = (-LRU_C * r) * softplus[:, cols]
        a = jnp.exp(log_a)
        a_s[:, cols] = a
        b_s[:, cols] = jnp.sqrt(-jnp.tanh(log_a) * (a * a + 1.0)) * (ig * xc)
        if j % g_per == g_per - 1:
            c = j // g_per
            g = _dot(h, win_ref[:, D_MODEL + c * NCH:D_MODEL + (c + 1) * NCH])
            hg_s[:, c * NCH:(c + 1) * NCH] = _silu(g).astype(BF16)

    ubuf[0:SUBLANES, :] = ubuf[TM:TM + SUBLANES, :]

    SW = 512
    n_rg = TM // SUBLANES
    row = lax.broadcasted_iota(jnp.int32, (SUBLANES, SW), 0)
    keep = {d: row >= d for d in (1, 2, 4)}
    hprev = [hcar[:, c * SW:(c + 1) * SW] for c in range(D_MODEL // SW)]
    ml_every = n_rg // (D_MODEL // NCH)
    for rg in range(n_rg):
        rows = slice(rg * SUBLANES, (rg + 1) * SUBLANES)
        for c in range(D_MODEL // SW):
            cols = slice(c * SW, (c + 1) * SW)
            a8 = a_s[rows, cols]
            b8 = b_s[rows, cols]
            for d in (1, 2, 4):
                b8 = jnp.where(keep[d], a8 * pltpu.roll(b8, d, 0) + b8, b8)
                a8 = jnp.where(keep[d], a8 * pltpu.roll(a8, d, 0), a8)
            hs = a8 * hprev[c] + b8
            b_s[rows, cols] = hs
            hprev[c] = jnp.broadcast_to(hs[SUBLANES - 1:SUBLANES, :], (SUBLANES, SW))
        if rg % ml_every == ml_every - 1:
            c = rg // ml_every
            ml = _dot(h, win_ref[:, 2 * D_MODEL + c * NCH:2 * D_MODEL + (c + 1) * NCH])
            ml_s[:, c * NCH:(c + 1) * NCH] = _sigmoid(ml)
    for c in range(D_MODEL // SW):
        hcar[:, c * SW:(c + 1) * SW] = hprev[c]

    for c in range(D_MODEL // NCH):
        cols = slice(c * NCH, (c + 1) * NCH)
        hg_s[:, cols] = (b_s[:, cols] * hg_s[:, cols].astype(F32)).astype(BF16)
    hg = hg_s[...]
    for c in range(D_MODEL // NCH):
        cols = slice(c * NCH, (c + 1) * NCH)
        b_out_ref[:, cols] = ml_s[:, cols] * _dot(hg, wo_ref[:, cols])


def _lru(h, w_in, w_o_lru, wl, layer):
    return pl.pallas_call(
        functools.partial(_lru_kernel, layer=layer),
        grid=(SEQ // TM,),
        in_specs=[
            pl.BlockSpec((TM, D_MODEL), lambda i: (i, 0)),
            _hbm(),
            _resident((CONV_K, D_MODEL)), _resident((1, D_MODEL)),
            _resident((LRU_BLOCKS, LRU_BLOCK_DIM, 2 * LRU_BLOCK_DIM)),
            _resident((1, D_MODEL)), _resident((1, D_MODEL)), _resident((1, D_MODEL)),
            _hbm(),
        ],
        out_specs=pl.BlockSpec((TM, D_MODEL), lambda i: (i, 0)),
        out_shape=jax.ShapeDtypeStruct((SEQ, D_MODEL), F32),
        scratch_shapes=[
            pltpu.VMEM((D_MODEL, 3 * D_MODEL), BF16),
            pltpu.VMEM((D_MODEL, D_MODEL), BF16),
            pltpu.VMEM((TM + SUBLANES, D_MODEL), F32),
            pltpu.VMEM((TM, D_MODEL), F32),
            pltpu.VMEM((TM, D_MODEL), F32),
            pltpu.VMEM((TM, D_MODEL), BF16),
            pltpu.VMEM((TM, D_MODEL), F32),
            pltpu.VMEM((SUBLANES, D_MODEL), F32),
        ] + _stage_scratch(),
        compiler_params=_params("arbitrary"),
        name="lru",
    )(h, w_in, wl["conv_w"], wl["conv_b"], wl["w_gates"], wl["b_rg"], wl["b_ig"],
      wl["lru_lambda"], w_o_lru)


def _final_kernel(x_ref, a_ref, b_ref, p_ref, wout_hbm, pg_ref, wpg_hbm, wple_hbm, ng_ref,
                  *refs, layer, last):
    n_out = 1 if last else 2
    out_refs = refs[:n_out]
    wout_ref, wpg_ref, wple_ref, stage, sem = refs[n_out:]

    @pl.when(pl.program_id(0) == 0)
    def _():
        _stage_weight(wout_hbm, layer, wout_ref, stage, sem)
        _stage_weight(wpg_hbm, layer, wpg_ref, stage, sem)
        _stage_weight(wple_hbm, layer, wple_ref, stage, sem)

    merged = (a_ref[...] + b_ref[...]).astype(BF16)
    x1 = x_ref[...] + _dot(merged, wout_ref[...])
    n1 = _rms(x1, pg_ref[...]).astype(BF16)
    gate = _sigmoid(_dot(n1, wpg_ref[...]))
    x2 = x1 + _dot(p_ref[...].astype(BF16), wple_ref[...]) * gate
    normed = _rms(x2, ng_ref[...])
    if last:
        out_refs[0][...] = normed
    else:
        out_refs[0][...] = x2
        out_refs[1][...] = normed.astype(BF16)


def _final(x, a, b, p, w_out, ple_gain, w_ple_gate, w_ple, next_gain, layer, last):
    row = pl.BlockSpec((TM, D_MODEL), lambda i: (i, 0))
    if last:
        out_specs = [row]
        out_shape = [jax.ShapeDtypeStruct((SEQ, D_MODEL), F32)]
    else:
        out_specs = [row, row]
        out_shape = [jax.ShapeDtypeStruct((SEQ, D_MODEL), F32),
                     jax.ShapeDtypeStruct((SEQ, D_MODEL), BF16)]
    return pl.pallas_call(
        functools.partial(_final_kernel, layer=layer, last=last),
        grid=(SEQ // TM,),
        in_specs=[
            row, row, row,
            pl.BlockSpec((None, None, TM, PLE_DIM), lambda i: (layer, 0, i, 0)),
            _hbm(), _resident((1, D_MODEL)), _hbm(), _hbm(), _resident((1, D_MODEL)),
        ],
        out_specs=out_specs,
        out_shape=out_shape,
        scratch_shapes=[
            pltpu.VMEM((D_MODEL, D_MODEL), BF16),
            pltpu.VMEM((D_MODEL, D_MODEL), BF16),
            pltpu.VMEM((PLE_DIM, D_MODEL), BF16),
        ] + _stage_scratch(),
        compiler_params=_params("arbitrary"),
        name="final",
    )(x, a, b, p, w_out, ple_gain, w_ple_gate, w_ple, next_gain)


def _swap_halves(w):
    half = QK_ROPE // 2
    return jnp.concatenate([w[..., half:], w[..., :half]], axis=-1)


def _layer_weights(l, q_a_norm, w_q_b, kv_a_norm, w_kv_b, conv_w, conv_b, w_rg, b_rg,
                   w_ig, b_ig, lru_lambda, ple_norm):
    wq = w_q_b[l].reshape(Q_LORA, N_HEADS, QK_NOPE + QK_ROPE)
    wq_r = wq[:, :, QK_NOPE:]
    wkv = w_kv_b[l].reshape(KV_LORA, N_HEADS, QK_NOPE + V_HEAD)
    row = lambda v: v[l].reshape(1, -1)
    return {
        "q_a_norm": row(q_a_norm), "kv_a_norm": row(kv_a_norm),
        "w_qn": wq[:, :, :QK_NOPE].reshape(Q_LORA, D_MODEL).astype(BF16),
        "w_qr": jnp.concatenate([wq_r, _swap_halves(wq_r)], axis=-1).reshape(Q_LORA, D_MODEL).astype(BF16),
        "w_kn": wkv[:, :, :QK_NOPE].reshape(KV_LORA, D_MODEL).astype(BF16),
        "w_vt": wkv[:, :, QK_NOPE:].reshape(KV_LORA, D_MODEL).T.astype(BF16),
        "conv_w": conv_w[l], "conv_b": row(conv_b),
        "w_gates": jnp.concatenate([w_rg[l], w_ig[l]], axis=-1).astype(BF16),
        "b_rg": row(b_rg), "b_ig": row(b_ig), "lru_lambda": row(lru_lambda),
        "ple_norm": row(ple_norm),
    }


def kernel(x, p, positions, attn_norm, w_in, q_a_norm, w_q_b, kv_a_norm, w_kv_b, conv_w, conv_b,
           w_rg, b_rg, w_ig, b_ig, lru_lambda, w_o_mla, w_o_lru, w_out, ple_norm, w_ple_gate,
           w_ple, final_norm):
    assert x.shape == (1, SEQ, D_MODEL) and p.shape == (DEPTH, 1, SEQ, PLE_DIM)
    cs = _rope_table(positions)
    xs = x.reshape(SEQ, D_MODEL)
    h = _first_norm(xs, attn_norm[0].reshape(1, D_MODEL))
    w_in_t = jnp.swapaxes(w_in, 1, 2)
    out = None
    for l in range(DEPTH):
        wl = _layer_weights(l, q_a_norm, w_q_b, kv_a_norm, w_kv_b, conv_w, conv_b, w_rg,
                            b_rg, w_ig, b_ig, lru_lambda, ple_norm)
        q, k, vt = _qkv(h, w_in_t, wl, cs, l)
        o = _attention(q, k, vt)
        a = _mla_out(h, o, w_in_t, w_o_mla, l)
        b = _lru(h, w_in_t, w_o_lru, wl, l)
        last = l == DEPTH - 1
        gain = (final_norm if last else attn_norm[l + 1]).reshape(1, D_MODEL)
        res = _final(xs, a, b, p, w_out, wl["ple_norm"], w_ple_gate, w_ple, gain, l, last)
        if last:
            out = res[0]
        else:
            xs, h = res
    return out.reshape(1, SEQ, D_MODEL)
```
